```python
import math
import jax, jax.numpy as jnp
from jax import lax
import numpy as np

D_MODEL = 2048
BATCH = 4
SEQ = 4096
DEPTH = 2

NORM_EPS = 1e-6
BLOCK = 128
ROPE_THETA = 10000.0

SSD_INNER = 2048
SSD_HEAD_DIM = 64
SSD_HEADS = SSD_INNER // SSD_HEAD_DIM
SSD_GROUPS = 4
SSD_REP = SSD_HEADS // SSD_GROUPS
SSD_STATE = 128
SSD_CONV = 4
SSD_CONV_CH = SSD_INNER + 2 * SSD_GROUPS * SSD_STATE

RET_HEADS = 8
RET_QK_DIM = 128
RET_V_DIM = 256
RET_QK = RET_HEADS * RET_QK_DIM
RET_V = RET_HEADS * RET_V_DIM

DIL_PATTERNS = ((128, 1), (512, 4), (2048, 16))
DIL_GROUPS = len(DIL_PATTERNS)
DIL_HEADS = 8
DIL_HEAD_DIM = 128
DIL_WIDTH = DIL_HEADS * DIL_HEAD_DIM

N_BRANCHES = 3

SPLITS = (
    SSD_INNER,
    SSD_CONV_CH,
    SSD_HEADS,
    RET_QK,
    RET_QK,
    RET_V,
    RET_V,
    DIL_GROUPS * 3 * DIL_WIDTH,
    DIL_WIDTH,
    N_BRANCHES * D_MODEL,
)
D_IN = sum(SPLITS)
SPLIT_POINTS = [int(v) for v in np.cumsum(SPLITS)[:-1]]

kernel_name = 'hybrid_ssd_retention_dilated_gated_merge'


def rms_norm(t, g):
    tf = t.astype(jnp.float32)
    tf = tf * lax.rsqrt(jnp.mean(tf * tf, axis=-1, keepdims=True) + NORM_EPS)
    return tf * g.astype(jnp.float32)


def rope_tables(s, dim):
    inv_freq = ROPE_THETA ** (-jnp.arange(0, dim, 2, dtype=jnp.float32) / dim)
    ang = jnp.arange(s, dtype=jnp.float32)[:, None] * inv_freq[None, :]
    return jnp.cos(ang), jnp.sin(ang)


def apply_rope(t, cos, sin):
    tf = t.astype(jnp.float32)
    half = tf.shape[-1] // 2
    t1, t2 = tf[..., :half], tf[..., half:]
    c = cos[None, :, None, :]
    s_ = sin[None, :, None, :]
    return jnp.concatenate([t1 * c - t2 * s_, t1 * s_ + t2 * c], axis=-1)


def causal_depthwise_conv(u, w, bias):
    c = u.shape[-1]
    out = lax.conv_general_dilated(
        u, w[:, None, :].astype(u.dtype), window_strides=(1,),
        padding=[(SSD_CONV - 1, 0)], dimension_numbers=('NWC', 'WIO', 'NWC'),
        feature_group_count=c)
    return out + bias.astype(u.dtype)


def ssd_chunked(xh, dt, a_neg, bm, cm):
    b, s, g, r, p = xh.shape
    n = bm.shape[-1]
    nc = s // BLOCK
    xc = xh.reshape(b, nc, BLOCK, g, r, p)
    dtc = dt.reshape(b, nc, BLOCK, g, r)
    bc = bm.reshape(b, nc, BLOCK, g, n)
    cc = cm.reshape(b, nc, BLOCK, g, n)
    acs = jnp.cumsum(dtc * a_neg, axis=2)
    causal = jnp.tril(jnp.ones((BLOCK, BLOCK), dtype=bool))
    seg = acs[:, :, :, None] - acs[:, :, None]
    decay = jnp.exp(jnp.where(causal[:, :, None, None], seg, -jnp.inf))
    cb = jnp.einsum('bclgn,bcsgn->bclsg', cc, bc)
    xdt = xc * dtc[..., None]
    y_diag = jnp.einsum('bclsgr,bcsgrp->bclgrp', cb[..., None] * decay, xdt)
    to_end = jnp.exp(acs[:, :, -1:] - acs)
    states = jnp.einsum('bcsgn,bcsgrp->bcgrpn', bc, xdt * to_end[..., None])
    chunk_decay = jnp.exp(acs[:, :, -1])

    def step(h, inp):
        st, dec = inp
        return h * dec[..., None, None] + st, h

    h0 = jnp.zeros((b, g, r, p, n), jnp.float32)
    _, h_in = lax.scan(step, h0, (jnp.moveaxis(states, 1, 0), jnp.moveaxis(chunk_decay, 1, 0)))
    h_in = jnp.moveaxis(h_in, 0, 1)
    y_off = jnp.einsum('bclgn,bcgrpn->bclgrp', cc, h_in) * jnp.exp(acs)[..., None]
    return (y_diag + y_off).reshape(b, s, g, r, p)


def ssd_branch(z, xbc, dt_raw, conv_w, conv_b, dt_bias, a_log, d_skip, norm_g):
    b, s, _ = xbc.shape
    xbc = jax.nn.silu(causal_depthwise_conv(xbc, conv_w, conv_b).astype(jnp.float32))
    xs = xbc[..., :SSD_INNER]
    bm = xbc[..., SSD_INNER:SSD_INNER + SSD_GROUPS * SSD_STATE].reshape(b, s, SSD_GROUPS, SSD_STATE)
    cm = xbc[..., SSD_INNER + SSD_GROUPS * SSD_STATE:].reshape(b, s, SSD_GROUPS, SSD_STATE)
    xh = xs.reshape(b, s, SSD_GROUPS, SSD_REP, SSD_HEAD_DIM)
    dt = jax.nn.softplus(dt_raw.astype(jnp.float32) + dt_bias.astype(jnp.float32))
    dt = dt.reshape(b, s, SSD_GROUPS, SSD_REP)
    a_neg = -jnp.exp(a_log.astype(jnp.float32)).reshape(SSD_GROUPS, SSD_REP)
    y = ssd_chunked(xh, dt, a_neg, bm, cm)
    y = y + xh * d_skip.astype(jnp.float32).reshape(SSD_GROUPS, SSD_REP)[..., None]
    y = y.reshape(b, s, SSD_INNER) * jax.nn.silu(z.astype(jnp.float32))
    return rms_norm(y, norm_g)


def retention_chunked(q, k, v, log_gamma):
    b, s, h, dk = q.shape
    dv = v.shape[-1]
    nc = s // BLOCK
    qc = q.reshape(b, nc, BLOCK, h, dk)
    kc = k.reshape(b, nc, BLOCK, h, dk)
    vc = v.reshape(b, nc, BLOCK, h, dv)
    idx = jnp.arange(BLOCK, dtype=jnp.float32)
    rel = idx[:, None] - idx[None, :]
    dmask = jnp.where((rel >= 0)[..., None], jnp.exp(jnp.maximum(rel, 0.0)[..., None] * log_gamma), 0.0)
    scores = jnp.einsum('bclhd,bcshd->bchls', qc, kc) * jnp.transpose(dmask, (2, 0, 1))
    inner = jnp.einsum('bchls,bcshd->bclhd', scores, vc)
    kdec = jnp.exp((BLOCK - 1 - idx)[:, None] * log_gamma)
    kv = jnp.einsum('bcshd,bcshe->bchde', kc * kdec[..., None], vc)
    chunk_decay = jnp.exp(BLOCK * log_gamma)

    def step(state, kv_c):
        return state * chunk_decay[:, None, None] + kv_c, state

    r0 = jnp.zeros((b, h, dk, dv), jnp.float32)
    _, r_in = lax.scan(step, r0, jnp.moveaxis(kv, 1, 0))
    r_in = jnp.moveaxis(r_in, 0, 1)
    qdec = jnp.exp((idx + 1.0)[:, None] * log_gamma)
    cross = jnp.einsum('bclhd,bchde->bclhe', qc * qdec[..., None], r_in)
    return (inner + cross).reshape(b, s, h, dv)


def retention_branch(q, k, v, gate, cos, sin):
    b, s, _ = q.shape
    qh = apply_rope(q.reshape(b, s, RET_HEADS, RET_QK_DIM), cos, sin)
    kh = apply_rope(k.reshape(b, s, RET_HEADS, RET_QK_DIM), cos, sin) * (RET_QK_DIM ** -0.5)
    vh = v.reshape(b, s, RET_HEADS, RET_V_DIM).astype(jnp.float32)
    log_gamma = jnp.log(1.0 - jnp.exp2(-5.0 - jnp.arange(RET_HEADS, dtype=jnp.float32)))
    o = retention_chunked(qh, kh, vh, log_gamma)
    o = o * lax.rsqrt(jnp.mean(o * o, axis=-1, keepdims=True) + NORM_EPS)
    return jax.nn.silu(gate.astype(jnp.float32)) * o.reshape(b, s, RET_V)


def dilated_window_attention(q, k, v, dilation, n_back):
    b, s, h, d = q.shape
    span = dilation * BLOCK
    s_pad = -(-s // span) * span
    pad = ((0, 0), (0, s_pad - s), (0, 0), (0, 0))
    L = s_pad // dilation
    nb = L // BLOCK

    def split(t):
        t = jnp.pad(t, pad).reshape(b, L, dilation, h, d).transpose(0, 2, 1, 3, 4)
        return t.reshape(b, dilation, nb, BLOCK, h, d)

    def with_prev(t):
        prev = jnp.pad(t, ((0, 0), (0, 0), (1, 0), (0, 0), (0, 0), (0, 0)))[:, :, :-1]
        return jnp.concatenate([prev, t], axis=3)

    qs = split(q)
    kb = with_prev(split(k))
    vb = with_prev(split(v))
    scores = jnp.einsum('brnqhd,brnkhd->brnhqk', qs, kb) * (d ** -0.5)
    qi = jnp.arange(BLOCK)[:, None]
    kj = jnp.arange(2 * BLOCK)[None, :]
    dist = qi + BLOCK - kj
    band = (dist >= 0) & (dist <= n_back)
    first = band & (kj >= BLOCK)
    mask = jnp.where((jnp.arange(nb) == 0)[:, None, None], first, band)
    scores = jnp.where(mask[None, None, :, None], scores, -jnp.inf)
    m = jnp.max(scores, axis=-1, keepdims=True)
    p = jnp.exp(scores - m)
    den = jnp.sum(p, axis=-1)
    o = jnp.einsum('brnhqk,brnkhd->brnqhd', p, vb) / jnp.moveaxis(den, 3, -1)[..., None]
    lse = jnp.moveaxis(m[..., 0] + jnp.log(den), 3, -1)
    o = o.reshape(b, dilation, L, h, d).transpose(0, 2, 1, 3, 4).reshape(b, s_pad, h, d)[:, :s]
    lse = lse.reshape(b, dilation, L, h).transpose(0, 2, 1, 3).reshape(b, s_pad, h)[:, :s]
    return o, lse


def dilated_branch(qkv, gate, cos, sin):
    b, s, _ = qkv.shape
    qkv = qkv.reshape(b, s, DIL_GROUPS, 3, DIL_HEADS, DIL_HEAD_DIM)
    outs, lses = [], []
    for gi, (window, dilation) in enumerate(DIL_PATTERNS):
        qg = apply_rope(qkv[:, :, gi, 0], cos, sin)
        kg = apply_rope(qkv[:, :, gi, 1], cos, sin)
        vg = qkv[:, :, gi, 2].astype(jnp.float32)
        o, lse = dilated_window_attention(qg, kg, vg, dilation, window // dilation)
        outs.append(o)
        lses.append(lse)
    o = jnp.stack(outs, axis=0)
    wts = jax.nn.softmax(jnp.stack(lses, axis=0), axis=0)
    o = jnp.sum(wts[..., None] * o, axis=0).reshape(b, s, DIL_WIDTH)
    return jax.nn.silu(gate.astype(jnp.float32)) * o


def setup_inputs(seed: int = 0) -> dict:
    key = jax.random.key(seed)
    ks = jax.random.split(key, 16)
    f32 = jnp.float32

    def nrm(k, shape, scale):
        return jax.random.normal(k, shape, f32) * scale

    x = jax.random.normal(ks[0], (BATCH, SEQ, D_MODEL), f32)
    norm_g = 1.0 + nrm(ks[1], (DEPTH, D_MODEL), 0.1)
    w_in = nrm(ks[2], (DEPTH, D_MODEL, D_IN), D_MODEL ** -0.5)
    conv_w = nrm(ks[3], (DEPTH, SSD_CONV, SSD_CONV_CH), SSD_CONV ** -0.5)
    conv_b = nrm(ks[4], (DEPTH, SSD_CONV_CH), 0.01)
    dt0 = jnp.exp(jax.random.uniform(ks[5], (DEPTH, SSD_HEADS), f32, math.log(1e-3), math.log(1e-1)))
    dt_bias = dt0 + jnp.log(-jnp.expm1(-dt0))
    a_log = jnp.log(jax.random.uniform(ks[6], (DEPTH, SSD_HEADS), f32, 1.0, 16.0))
    d_skip = 1.0 + nrm(ks[7], (DEPTH, SSD_HEADS), 0.1)
    ssd_norm_g = 1.0 + nrm(ks[8], (DEPTH, SSD_INNER), 0.1)
    w_o_ssd = nrm(ks[9], (DEPTH, SSD_INNER, D_MODEL), SSD_INNER ** -0.5)
    w_o_ret = nrm(ks[10], (DEPTH, RET_V, D_MODEL), RET_V ** -0.5)
    w_o_dil = nrm(ks[11], (DEPTH, DIL_WIDTH, D_MODEL), DIL_WIDTH ** -0.5)
    w_out = nrm(ks[12], (DEPTH, D_MODEL, D_MODEL), D_MODEL ** -0.5)
    final_norm_g = 1.0 + nrm(ks[13], (D_MODEL,), 0.1)
    return {'x': x, 'norm_g': norm_g, 'w_in': w_in, 'conv_w': conv_w, 'conv_b': conv_b,
            'dt_bias': dt_bias, 'a_log': a_log, 'd_skip': d_skip, 'ssd_norm_g': ssd_norm_g,
            'w_o_ssd': w_o_ssd, 'w_o_ret': w_o_ret, 'w_o_dil': w_o_dil, 'w_out': w_out,
            'final_norm_g': final_norm_g}


def reference(x, norm_g, w_in, conv_w, conv_b, dt_bias, a_log, d_skip, ssd_norm_g,
              w_o_ssd, w_o_ret, w_o_dil, w_out, final_norm_g):
    b, s, _ = x.shape
    cos_r, sin_r = rope_tables(s, RET_QK_DIM)
    cos_d, sin_d = rope_tables(s, DIL_HEAD_DIM)
    for layer in range(DEPTH):
        h = rms_norm(x, norm_g[layer]).astype(x.dtype)
        proj = h @ w_in[layer]
        (z, xbc, dt_raw, rq, rk, rv, rg, dqkv, dg, mg) = jnp.split(proj, SPLIT_POINTS, axis=-1)
        y_a = ssd_branch(z, xbc, dt_raw, conv_w[layer], conv_b[layer], dt_bias[layer],
                         a_log[layer], d_skip[layer], ssd_norm_g[layer]).astype(x.dtype)
        y_b = retention_branch(rq, rk, rv, rg, cos_r, sin_r).astype(x.dtype)
        y_c = dilated_branch(dqkv, dg, cos_d, sin_d).astype(x.dtype)
        gates = jax.nn.sigmoid(mg.astype(jnp.float32)).reshape(b, s, N_BRANCHES, D_MODEL)
        merged = (gates[:, :, 0] * (y_a @ w_o_ssd[layer]).astype(jnp.float32)
                  + gates[:, :, 1] * (y_b @ w_o_ret[layer]).astype(jnp.float32)
                  + gates[:, :, 2] * (y_c @ w_o_dil[layer]).astype(jnp.float32))
        x = x + merged.astype(x.dtype) @ w_out[layer]
    return rms_norm(x, final_norm_g).astype(x.dtype)
```

```python
import functools
import math

import numpy as np
import jax
import jax.numpy as jnp
from jax import lax
from jax.experimental import pallas as pl
from jax.experimental.pallas import tpu as pltpu

F32 = jnp.float32
BF16 = jnp.bfloat16

NORM_EPS = 1e-6
CHUNK = 128
ROPE_THETA = 10000.0

SSD_INNER = 2048
SSD_HEAD_DIM = 64
SSD_HEADS = 32
SSD_GROUPS = 4
SSD_REP = SSD_HEADS // SSD_GROUPS
SSD_STATE = 128
SSD_CONV = 4
SSD_BC = SSD_GROUPS * SSD_STATE

RET_HEADS = 8
RET_QK_DIM = 128
RET_V_DIM = 256
RET_QK = RET_HEADS * RET_QK_DIM
RET_V = RET_HEADS * RET_V_DIM

DIL_DILATIONS = (1, 4, 16)
DIL_BACK = 128
DIL_HEADS = 8
DIL_HEAD_DIM = 128
DIL_WIDTH = DIL_HEADS * DIL_HEAD_DIM

LANES = 128
HALO = 8
VMEM_LIMIT = 56 * 1024 * 1024

COL_Z, COL_XS, COL_RV, COL_RG, COL_MG = 0, 2048, 4096, 6144, 8192
COL_DQKV0 = 14336
COL_RQ, COL_RK, COL_DG, COL_B, COL_C = 17408, 18432, 19456, 20480, 20992
N_MAIN = 21504


def _cparams(sem):
    return pltpu.CompilerParams(dimension_semantics=sem, vmem_limit_bytes=VMEM_LIMIT)


def _sigmoid(v):
    return 1.0 / (1.0 + jnp.exp(-v))


def _silu(v):
    return v * _sigmoid(v)


def _dot(a, b):
    return jnp.dot(a, b, preferred_element_type=F32)


def _dot_nt(a, b):
    return lax.dot_general(a, b, (((1,), (1,)), ((), ())), preferred_element_type=F32)


def _norm_kernel(x_ref, g_ref, o_ref):
    x = x_ref[...]
    r = lax.rsqrt(jnp.mean(x * x, axis=-1, keepdims=True) + NORM_EPS)
    o_ref[...] = (x * r * g_ref[...]).astype(o_ref.dtype)


def _rms_norm(x2d, g, out_dtype, tm=512):
    m, d = x2d.shape
    return pl.pallas_call(
        _norm_kernel,
        grid=(m // tm,),
        in_specs=[pl.BlockSpec((tm, d), lambda i: (i, 0)),
                  pl.BlockSpec((1, d), lambda i: (0, 0))],
        out_specs=pl.BlockSpec((tm, d), lambda i: (i, 0)),
        out_shape=jax.ShapeDtypeStruct((m, d), out_dtype),
        compiler_params=_cparams(("parallel",)),
        name="rms_norm",
    )(x2d, g.reshape(1, d))


def _mm_kernel(a_ref, w_ref, o_ref):
    o_ref[...] = _dot(a_ref[...], w_ref[...]).astype(o_ref.dtype)


def _matmul(a, w, out_dtype, tm, tn, name):
    m, k = a.shape
    n = w.shape[1]
    return pl.pallas_call(
        _mm_kernel,
        grid=(n // tn, m // tm),
        in_specs=[pl.BlockSpec((tm, k), lambda j, i: (i, 0)),
                  pl.BlockSpec((k, tn), lambda j, i: (0, j))],
        out_specs=pl.BlockSpec((tm, tn), lambda j, i: (i, j)),
        out_shape=jax.ShapeDtypeStruct((m, n), out_dtype),
        compiler_params=_cparams(("parallel", "parallel")),
        name=name,
    )(a, w)


def _matmul_strided_rows(a_view, w, batch, dil, tm, tn, name):
    rows, kd = a_view.shape
    k = kd // dil
    n = w.shape[1]
    per_b = rows // batch
    blocks_per_b = per_b // tm

    def out_map(j, r, i):
        return ((i // blocks_per_b) * dil + r) * blocks_per_b + i % blocks_per_b, j

    out = pl.pallas_call(
        _mm_kernel,
        grid=(n // tn, dil, rows // tm),
        in_specs=[pl.BlockSpec((tm, k), lambda j, r, i: (i, r)),
                  pl.BlockSpec((k, tn), lambda j, r, i: (0, j))],
        out_specs=pl.BlockSpec((tm, tn), out_map),
        out_shape=jax.ShapeDtypeStruct((rows * dil, n), BF16),
        compiler_params=_cparams(("parallel", "parallel", "parallel")),
        name=name,
    )(a_view, w)
    return out.reshape(batch, dil, per_b, n)


def _ssd_kernel(z_ref, xs_ref, bm_ref, cm_ref, dt_ref,
                cwx_ref, cwb_ref, cwc_ref, cbx_ref, cbb_ref, cbc_ref,
                dtb_ref, alog_ref, dskip_ref, ng_ref,
                o_ref, xbuf, bbuf, cbuf, state, ybuf):
    c = pl.program_id(1)

    @pl.when(c == 0)
    def _():
        xbuf[0:HALO, :] = jnp.zeros((HALO, SSD_INNER), F32)
        bbuf[0:HALO, :] = jnp.zeros((HALO, SSD_BC), F32)
        cbuf[0:HALO, :] = jnp.zeros((HALO, SSD_BC), F32)
        state[...] = jnp.zeros_like(state)

    def conv_silu(in_ref, buf, w_ref, b_ref):
        buf[HALO:HALO + CHUNK, :] = in_ref[...].astype(F32)
        acc = b_ref[...] + w_ref[SSD_CONV - 1:SSD_CONV, :] * buf[HALO:HALO + CHUNK, :]
        for back in range(1, SSD_CONV):
            tap = SSD_CONV - 1 - back
            acc = acc + w_ref[tap:tap + 1, :] * buf[HALO - back:HALO - back + CHUNK, :]
        buf[0:HALO, :] = buf[CHUNK:CHUNK + HALO, :]
        return _silu(acc)

    xs = conv_silu(xs_ref, xbuf, cwx_ref, cbx_ref)
    bm = conv_silu(bm_ref, bbuf, cwb_ref, cbb_ref)
    cm = conv_silu(cm_ref, cbuf, cwc_ref, cbc_ref)
    xs_b = xs.astype(BF16)

    v = dt_ref[...] + dtb_ref[...]
    dt = jnp.maximum(v, 0.0) + jnp.log1p(jnp.exp(-jnp.abs(v)))
    a_neg = -jnp.exp(alog_ref[...])
    da = dt * a_neg
    row = lax.broadcasted_iota(jnp.int32, (CHUNK, CHUNK), 0)
    col = lax.broadcasted_iota(jnp.int32, (CHUNK, CHUNK), 1)
    causal = row >= col
    tril = jnp.where(causal, 1.0, 0.0).astype(F32)
    acs = jnp.dot(tril, da, preferred_element_type=F32, precision=lax.Precision.HIGHEST)
    acs_t = acs.T
    dt_t = dt.T
    total = acs[CHUNK - 1:CHUNK, :]
    w_t = (dt * jnp.exp(total - acs)).T

    for g in range(SSD_GROUPS):
        gs = slice(g * SSD_STATE, (g + 1) * SSD_STATE)
        bm_g = bm[:, gs]
        cm_g = cm[:, gs]
        bm_gb = bm_g.astype(BF16)
        cm_gb = cm_g.astype(BF16)
        cb = _dot_nt(cm_gb, bm_gb)
        bm_gt = bm_g.T
        for r in range(SSD_REP):
            h = g * SSD_REP + r
            hs = slice(h * SSD_HEAD_DIM, (h + 1) * SSD_HEAD_DIM)
            a_col = jnp.broadcast_to(acs[:, h:h + 1], (CHUNK, CHUNK))
            a_row = acs_t[h:h + 1, :]
            decay = jnp.exp(jnp.where(causal, a_col - a_row, -jnp.inf))
            m_h = (cb * decay * dt_t[h:h + 1, :]).astype(BF16)
            xs_h = xs_b[:, hs]
            y_diag = _dot(m_h, xs_h)
            e_col = jnp.exp(a_col)
            st_old = state[h]
            y_off = _dot(cm_gb, st_old.astype(BF16)) * e_col[:, :SSD_HEAD_DIM]
            st_new = _dot((bm_gt * w_t[h:h + 1, :]).astype(BF16), xs_h)
            state[h] = st_old * e_col[CHUNK - 1:CHUNK, :SSD_HEAD_DIM] + st_new
            ybuf[:, hs] = y_diag + y_off

    y = (ybuf[...] + xs * dskip_ref[...]) * _silu(z_ref[...].astype(F32))
    r = lax.rsqrt(jnp.mean(y * y, axis=-1, keepdims=True) + NORM_EPS)
    o_ref[...] = (y * r * ng_ref[...]).astype(o_ref.dtype)


def _ssd_branch(proj, dt_raw, conv_w, conv_b, dt_bias, a_log, d_skip, norm_g):
    b, s, _ = proj.shape
    nc = s // CHUNK

    def col(width, offset):
        return pl.BlockSpec((None, CHUNK, width), lambda bi, ci: (bi, ci, offset // width))

    def const(shape):
        return pl.BlockSpec(shape, lambda bi, ci: (0,) * len(shape))

    pad = LANES - SSD_HEADS
    cwx, cwb, cwc = (conv_w[:, :SSD_INNER], conv_w[:, SSD_INNER:SSD_INNER + SSD_BC],
                     conv_w[:, SSD_INNER + SSD_BC:])
    cbx, cbb, cbc = (conv_b[None, :SSD_INNER], conv_b[None, SSD_INNER:SSD_INNER + SSD_BC],
                     conv_b[None, SSD_INNER + SSD_BC:])
    dtb = jnp.pad(dt_bias, (0, pad))[None, :]
    alog = jnp.pad(a_log, (0, pad))[None, :]
    dskip = jnp.repeat(d_skip, SSD_HEAD_DIM)[None, :]
    return pl.pallas_call(
        _ssd_kernel,
        grid=(b, nc),
        in_specs=[col(SSD_INNER, COL_Z), col(SSD_INNER, COL_XS), col(SSD_BC, COL_B), col(SSD_BC, COL_C),
                  pl.BlockSpec((None, CHUNK, LANES), lambda bi, ci: (bi, ci, 0)),
                  const((SSD_CONV, SSD_INNER)), const((SSD_CONV, SSD_BC)), const((SSD_CONV, SSD_BC)),
                  const((1, SSD_INNER)), const((1, SSD_BC)), const((1, SSD_BC)),
                  const((1, LANES)), const((1, LANES)), const((1, SSD_INNER)), const((1, SSD_INNER))],
        out_specs=pl.BlockSpec((None, CHUNK, SSD_INNER), lambda bi, ci: (bi, ci, 0)),
        out_shape=jax.ShapeDtypeStruct((b, s, SSD_INNER), BF16),
        scratch_shapes=[pltpu.VMEM((CHUNK + HALO, SSD_INNER), F32),
                        pltpu.VMEM((CHUNK + HALO, SSD_BC), F32),
                        pltpu.VMEM((CHUNK + HALO, SSD_BC), F32),
                        pltpu.VMEM((SSD_HEADS, SSD_STATE, SSD_HEAD_DIM), F32),
                        pltpu.VMEM((CHUNK, SSD_INNER), F32)],
        compiler_params=_cparams(("parallel", "arbitrary")),
        name="ssd_mixer",
    )(proj, proj, proj, proj, dt_raw, cwx, cwb, cwc, cbx, cbb, cbc, dtb, alog, dskip, norm_g[None, :])


def _rope(t, cosf, sinf):
    return t * cosf + pltpu.roll(t, DIL_HEAD_DIM // 2, 1) * sinf


def _rope_tables(s):
    inv_freq = ROPE_THETA ** (-jnp.arange(0, RET_QK_DIM, 2, dtype=F32) / RET_QK_DIM)
    ang = jnp.arange(s, dtype=F32)[:, None] * inv_freq[None, :]
    cos, sin = jnp.cos(ang), jnp.sin(ang)
    return jnp.concatenate([cos, cos], axis=-1), jnp.concatenate([-sin, sin], axis=-1)


_LOG_GAMMA = [float(np.log(np.float32(1.0) - np.exp2(np.float32(-5.0 - h)))) for h in range(RET_HEADS)]


def _ret_kernel(q_ref, k_ref, v_ref, g_ref, cos_ref, sin_ref, o_ref, state, dmask, qdec, kdec):
    c = pl.program_id(1)

    @pl.when(c == 0)
    def _():
        state[...] = jnp.zeros_like(state)
        li = lax.broadcasted_iota(jnp.int32, (CHUNK, CHUNK), 0).astype(F32)
        si = lax.broadcasted_iota(jnp.int32, (CHUNK, CHUNK), 1).astype(F32)
        rel = li - si
        for h in range(RET_HEADS):
            lg = _LOG_GAMMA[h]
            dmask[h] = jnp.where(rel >= 0, jnp.exp(jnp.maximum(rel, 0.0) * lg), 0.0)
            qdec[h] = jnp.exp((li + 1.0) * lg)
            kdec[h] = jnp.exp((CHUNK - 1.0 - li) * lg)

    cosf = cos_ref[...]
    sinf = sin_ref[...]
    for h in range(RET_HEADS):
        qs = slice(h * RET_QK_DIM, (h + 1) * RET_QK_DIM)
        vs = slice(h * RET_V_DIM, (h + 1) * RET_V_DIM)
        qr = _rope(q_ref[:, qs].astype(F32), cosf, sinf)
        kr = _rope(k_ref[:, qs].astype(F32), cosf, sinf) * (RET_QK_DIM ** -0.5)
        v_h = v_ref[:, vs]
        scores = _dot_nt(qr.astype(BF16), kr.astype(BF16)) * dmask[h]
        inner = _dot(scores.astype(BF16), v_h)
        st_old = state[h]
        cross = _dot((qr * qdec[h]).astype(BF16), st_old.astype(BF16))
        kv = _dot((kr * kdec[h]).T.astype(BF16), v_h)
        state[h] = st_old * math.exp(CHUNK * _LOG_GAMMA[h]) + kv
        o = inner + cross
        o = o * lax.rsqrt(jnp.mean(o * o, axis=-1, keepdims=True) + NORM_EPS)
        o_ref[:, vs] = (_silu(g_ref[:, vs].astype(F32)) * o).astype(o_ref.dtype)


def _ret_branch(proj, cosf, sinf):
    b, s, _ = proj.shape
    nc = s // CHUNK

    def col(width, offset):
        return pl.BlockSpec((None, CHUNK, width), lambda bi, ci: (bi, ci, offset // width))

    tab = pl.BlockSpec((CHUNK, RET_QK_DIM), lambda bi, ci: (ci, 0))
    return pl.pallas_call(
        _ret_kernel,
        grid=(b, nc),
        in_specs=[col(RET_QK, COL_RQ), col(RET_QK, COL_RK), col(RET_V, COL_RV), col(RET_V, COL_RG), tab, tab],
        out_specs=pl.BlockSpec((None, CHUNK, RET_V), lambda bi, ci: (bi, ci, 0)),
        out_shape=jax.ShapeDtypeStruct((b, s, RET_V), BF16),
        scratch_shapes=[pltpu.VMEM((RET_HEADS, RET_QK_DIM, RET_V_DIM), F32),
                        pltpu.VMEM((RET_HEADS, CHUNK, CHUNK), F32),
                        pltpu.VMEM((RET_HEADS, CHUNK, CHUNK), F32),
                        pltpu.VMEM((RET_HEADS, CHUNK, CHUNK), F32)],
        compiler_params=_cparams(("parallel", "arbitrary")),
        name="retention_mixer",
    )(proj, proj, proj, proj, cosf, sinf)


def _dil_kernel(q_ref, k_ref, v_ref, cos_ref, sin_ref, o_ref, lse_ref, kprev, vprev):
    n = pl.program_id(2)

    @pl.when(n == 0)
    def _():
        kprev[...] = jnp.zeros_like(kprev)
        vprev[...] = jnp.zeros_like(vprev)

    cosf = cos_ref[...]
    sinf = sin_ref[...]
    qi = lax.broadcasted_iota(jnp.int32, (CHUNK, CHUNK), 0)
    kj = lax.broadcasted_iota(jnp.int32, (CHUNK, CHUNK), 1)
    mask_cur = kj <= qi
    mask_prev = jnp.logical_and(kj >= qi, n > 0)
    lse_all = jnp.zeros((CHUNK, LANES), F32)
    for h in range(DIL_HEADS):
        hs = slice(h * DIL_HEAD_DIM, (h + 1) * DIL_HEAD_DIM)
        qb = (_rope(q_ref[:, hs].astype(F32), cosf, sinf) * (DIL_HEAD_DIM ** -0.5)).astype(BF16)
        kb = _rope(k_ref[:, hs].astype(F32), cosf, sinf).astype(BF16)
        s_cur = jnp.where(mask_cur, _dot_nt(qb, kb), -jnp.inf)
        s_prev = jnp.where(mask_prev, _dot_nt(qb, kprev[:, hs]), -jnp.inf)
        m = jnp.maximum(jnp.max(s_cur, axis=-1, keepdims=True), jnp.max(s_prev, axis=-1, keepdims=True))
        p_cur = jnp.exp(s_cur - m)
        p_prev = jnp.exp(s_prev - m)
        den = jnp.sum(p_cur, axis=-1, keepdims=True) + jnp.sum(p_prev, axis=-1, keepdims=True)
        acc = _dot(p_cur.astype(BF16), v_ref[:, hs]) + _dot(p_prev.astype(BF16), vprev[:, hs])
        o_ref[:, hs] = (acc / den).astype(o_ref.dtype)
        lse_all = jnp.where(kj == h, m + jnp.log(den), lse_all)
        kprev[:, hs] = kb
    vprev[...] = v_ref[...]
    lse_ref[...] = lse_all


def _dil_group(qkv, col0, dil, cos_t, sin_t):
    b, _, sub, _ = qkv.shape
    nb = sub // CHUNK
    cb = col0 // DIL_WIDTH

    def part(which):
        return pl.BlockSpec((None, None, CHUNK, DIL_WIDTH), lambda bi, r, n: (bi, r, n, cb + which))

    tab = pl.BlockSpec((None, CHUNK, DIL_HEAD_DIM), lambda bi, r, n: (r, n, 0))
    return pl.pallas_call(
        _dil_kernel,
        grid=(b, dil, nb),
        in_specs=[part(0), part(1), part(2), tab, tab],
        out_specs=[pl.BlockSpec((None, CHUNK, DIL_WIDTH), lambda bi, r, n: (bi, n, r)),
                   pl.BlockSpec((None, CHUNK, LANES), lambda bi, r, n: (bi, n, r))],
        out_shape=[jax.ShapeDtypeStruct((b, sub, dil * DIL_WIDTH), BF16),
                   jax.ShapeDtypeStruct((b, sub, dil * LANES), F32)],
        scratch_shapes=[pltpu.VMEM((CHUNK, DIL_WIDTH), BF16), pltpu.VMEM((CHUNK, DIL_WIDTH), BF16)],
        compiler_params=_cparams(("parallel", "parallel", "arbitrary")),
        name=f"dilated_attention_d{dil}",
    )(qkv, qkv, qkv, cos_t, sin_t)


def _dil_combine_kernel(o0_ref, o1_ref, o2_ref, l0_ref, l1_ref, l2_ref, g_ref, y_ref):
    l0, l1, l2 = l0_ref[...], l1_ref[...], l2_ref[...]
    m = jnp.maximum(jnp.maximum(l0, l1), l2)
    e0, e1, e2 = jnp.exp(l0 - m), jnp.exp(l1 - m), jnp.exp(l2 - m)
    inv = 1.0 / (e0 + e1 + e2)
    w0, w1, w2 = e0 * inv, e1 * inv, e2 * inv
    rows = l0.shape[0]
    for h in range(DIL_HEADS):
        hs = slice(h * DIL_HEAD_DIM, (h + 1) * DIL_HEAD_DIM)

        def lane(w):
            return jnp.broadcast_to(w[:, h:h + 1], (rows, DIL_HEAD_DIM))

        o = (lane(w0) * o0_ref[:, hs].astype(F32) + lane(w1) * o1_ref[:, hs].astype(F32)
             + lane(w2) * o2_ref[:, hs].astype(F32))
        y_ref[:, hs] = (_silu(g_ref[:, hs].astype(F32)) * o).astype(y_ref.dtype)


def _dil_combine(outs, lses, proj2d, tm=512):
    m = proj2d.shape[0]
    o_spec = pl.BlockSpec((tm, DIL_WIDTH), lambda i: (i, 0))
    l_spec = pl.BlockSpec((tm, LANES), lambda i: (i, 0))
    return pl.pallas_call(
        _dil_combine_kernel,
        grid=(m // tm,),
        in_specs=[o_spec, o_spec, o_spec, l_spec, l_spec, l_spec,
                  pl.BlockSpec((tm, DIL_WIDTH), lambda i: (i, COL_DG // DIL_WIDTH))],
        out_specs=o_spec,
        out_shape=jax.ShapeDtypeStruct((m, DIL_WIDTH), BF16),
        compiler_params=_cparams(("parallel",)),
        name="dilated_combine",
    )(*outs, *lses, proj2d)


def _merge_kernel(ya_ref, yb_ref, yc_ref, wa_ref, wb_ref, wc_ref, ga_ref, gb_ref, gc_ref, o_ref):
    acc = _sigmoid(ga_ref[...].astype(F32)) * _dot(ya_ref[...], wa_ref[...])
    acc = acc + _sigmoid(gb_ref[...].astype(F32)) * _dot(yb_ref[...], wb_ref[...])
    acc = acc + _sigmoid(gc_ref[...].astype(F32)) * _dot(yc_ref[...], wc_ref[...])
    o_ref[...] = acc.astype(o_ref.dtype)


def _merge(ya, yb, yc, wa, wb, wc, proj2d, tm=512, tn=1024):
    m = ya.shape[0]
    d = wa.shape[1]

    def act(width):
        return pl.BlockSpec((tm, width), lambda j, i: (i, 0))

    def wgt(width):
        return pl.BlockSpec((width, tn), lambda j, i: (0, j))

    def gate(branch):
        return pl.BlockSpec((tm, tn), lambda j, i: (i, (COL_MG + branch * d) // tn + j))

    return pl.pallas_call(
        _merge_kernel,
        grid=(d // tn, m // tm),
        in_specs=[act(SSD_INNER), act(RET_V), act(DIL_WIDTH), wgt(SSD_INNER), wgt(RET_V), wgt(DIL_WIDTH),
                  gate(0), gate(1), gate(2)],
        out_specs=pl.BlockSpec((tm, tn), lambda j, i: (i, j)),
        out_shape=jax.ShapeDtypeStruct((m, d), BF16),
        compiler_params=_cparams(("parallel", "parallel")),
        name="gated_merge",
    )(ya, yb, yc, wa, wb, wc, proj2d, proj2d, proj2d)


def _out_kernel(a_ref, w_ref, x_ref, g_ref, xo_ref, ho_ref):
    x = x_ref[...] + _dot(a_ref[...], w_ref[...])
    xo_ref[...] = x
    r = lax.rsqrt(jnp.mean(x * x, axis=-1, keepdims=True) + NORM_EPS)
    ho_ref[...] = (x * r * g_ref[...]).astype(ho_ref.dtype)


def _out_proj(merged, w_out, x2d, g_next, norm_dtype, tm=512):
    m, d = x2d.shape
    row = pl.BlockSpec((tm, d), lambda i: (i, 0))
    return pl.pallas_call(
        _out_kernel,
        grid=(m // tm,),
        in_specs=[row, pl.BlockSpec((d, d), lambda i: (0, 0)), row, pl.BlockSpec((1, d), lambda i: (0, 0))],
        out_specs=[row, row],
        out_shape=[jax.ShapeDtypeStruct((m, d), F32), jax.ShapeDtypeStruct((m, d), norm_dtype)],
        compiler_params=_cparams(("parallel",)),
        name="out_proj_residual_norm",
    )(merged, w_out, x2d, g_next.reshape(1, d))


def _split_w_in(w):
    z, xs, bmat, cmat = w[:, 0:2048], w[:, 2048:4096], w[:, 4096:4608], w[:, 4608:5120]
    dt = w[:, 5120:5152]
    rq, rk, rv, rg = w[:, 5152:6176], w[:, 6176:7200], w[:, 7200:9248], w[:, 9248:11296]
    dq0, dq1, dq2 = w[:, 11296:14368], w[:, 14368:17440], w[:, 17440:20512]
    dg, mg = w[:, 20512:21536], w[:, 21536:27680]
    main = jnp.concatenate([z, xs, rv, rg, mg, dq0, rq, rk, dg, bmat, cmat], axis=1).astype(BF16)
    dt = jnp.pad(dt, ((0, 0), (0, LANES - SSD_HEADS))).astype(BF16)
    return main, dt, dq1.astype(BF16), dq2.astype(BF16)


def kernel(x, norm_g, w_in, conv_w, conv_b, dt_bias, a_log, d_skip, ssd_norm_g,
           w_o_ssd, w_o_ret, w_o_dil, w_out, final_norm_g):
    b, s, d = x.shape
    depth = w_in.shape[0]
    m = b * s
    assert s % (max(DIL_DILATIONS) * CHUNK) == 0 and d == 2048
    cosf, sinf = _rope_tables(s)
    tabs = []
    for dil in DIL_DILATIONS:
        sub = s // dil
        tabs.append((cosf.reshape(sub, dil, -1).transpose(1, 0, 2), sinf.reshape(sub, dil, -1).transpose(1, 0, 2)))

    x2d = x.reshape(m, d)
    h = _rms_norm(x2d, norm_g[0], BF16)
    for layer in range(depth):
        w_main, w_dt, w_dq1, w_dq2 = _split_w_in(w_in[layer])
        proj2d = _matmul(h, w_main, BF16, 1024, 1024, "in_proj")
        dt_raw = _matmul(h, w_dt, F32, 1024, LANES, "dt_proj")
        proj = proj2d.reshape(b, s, N_MAIN)

        y_a = _ssd_branch(proj, dt_raw.reshape(b, s, LANES), conv_w[layer], conv_b[layer], dt_bias[layer],
                          a_log[layer], d_skip[layer], ssd_norm_g[layer])
        y_b = _ret_branch(proj, cosf, sinf)

        outs, lses = [], []
        for gi, dil in enumerate(DIL_DILATIONS):
            if dil == 1:
                qkv, col0 = proj.reshape(b, 1, s, N_MAIN), COL_DQKV0
            else:
                h_view = h.reshape(m // dil, dil * d)
                w_g = w_dq1 if gi == 1 else w_dq2
                qkv, col0 = _matmul_strided_rows(h_view, w_g, b, dil, min(1024, s // dil), 1024,
                                                 f"dqkv_proj_d{dil}"), 0
            o, lse = _dil_group(qkv, col0, dil, tabs[gi][0], tabs[gi][1])
            outs.append(o.reshape(m, DIL_WIDTH))
            lses.append(lse.reshape(m, LANES))
        y_c = _dil_combine(outs, lses, proj2d)

        merged = _merge(y_a.reshape(m, SSD_INNER), y_b.reshape(m, RET_V), y_c,
                        w_o_ssd[layer].astype(BF16), w_o_ret[layer].astype(BF16), w_o_dil[layer].astype(BF16),
                        proj2d)
        last = layer == depth - 1
        g_next = final_norm_g if last else norm_g[layer + 1]
        x2d, h = _out_proj(merged, w_out[layer].astype(BF16), x2d, g_next, F32 if last else BF16)
    return h.reshape(b, s, d)
```

```python
import functools
import math

import numpy as np
import jax
import jax.numpy as jnp
from jax import lax
from jax.experimental import pallas as pl
from jax.experimental.pallas import tpu as pltpu

F32 = jnp.float32
BF16 = jnp.bfloat16

NORM_EPS = 1e-6
CHUNK = 128
ROPE_THETA = 10000.0

SSD_INNER = 2048
SSD_HEAD_DIM = 64
SSD_HEADS = 32
SSD_GROUPS = 4
SSD_REP = SSD_HEADS // SSD_GROUPS
SSD_STATE = 128
SSD_CONV = 4
SSD_BC = SSD_GROUPS * SSD_STATE

RET_HEADS = 8
RET_QK_DIM = 128
RET_V_DIM = 256
RET_QK = RET_HEADS * RET_QK_DIM
RET_V = RET_HEADS * RET_V_DIM

DIL_DILATIONS = (1, 4, 16)
DIL_BACK = 128
DIL_HEADS = 8
DIL_HEAD_DIM = 128
DIL_WIDTH = DIL_HEADS * DIL_HEAD_DIM
DIL_HEAD_GROUP = 4

LANES = 128
HALO = 8
VMEM_LIMIT = 56 * 1024 * 1024

COL_Z, COL_XS, COL_RV, COL_RG, COL_MG = 0, 2048, 4096, 6144, 8192
COL_DQKV0 = 14336
COL_RQ, COL_RK, COL_DG, COL_B, COL_C = 17408, 18432, 19456, 20480, 20992
N_MAIN = 21504


def _cparams(sem):
    return pltpu.CompilerParams(dimension_semantics=sem, vmem_limit_bytes=VMEM_LIMIT)


def _sigmoid(v):
    return 1.0 / (1.0 + jnp.exp(-v))


def _silu(v):
    return v * _sigmoid(v)


def _dot(a, b):
    return jnp.dot(a, b, preferred_element_type=F32)


def _dot_nt(a, b):
    return lax.dot_general(a, b, (((1,), (1,)), ((), ())), preferred_element_type=F32)


def _norm_kernel(x_ref, g_ref, o_ref):
    x = x_ref[...]
    r = lax.rsqrt(jnp.mean(x * x, axis=-1, keepdims=True) + NORM_EPS)
    o_ref[...] = (x * r * g_ref[...]).astype(o_ref.dtype)


def _rms_norm(x2d, g, out_dtype, tm=512):
    m, d = x2d.shape
    return pl.pallas_call(
        _norm_kernel,
        grid=(m // tm,),
        in_specs=[pl.BlockSpec((tm, d), lambda i: (i, 0)),
                  pl.BlockSpec((1, d), lambda i: (0, 0))],
        out_specs=pl.BlockSpec((tm, d), lambda i: (i, 0)),
        out_shape=jax.ShapeDtypeStruct((m, d), out_dtype),
        compiler_params=_cparams(("parallel",)),
        name="rms_norm",
    )(x2d, g.reshape(1, d))


def _mm_kernel(a_ref, w_ref, o_ref):
    o_ref[...] = _dot(a_ref[...], w_ref[...]).astype(o_ref.dtype)


def _matmul(a, w, out_dtype, tm, tn, name):
    m, k = a.shape
    n = w.shape[1]
    return pl.pallas_call(
        _mm_kernel,
        grid=(n // tn, m // tm),
        in_specs=[pl.BlockSpec((tm, k), lambda j, i: (i, 0)),
                  pl.BlockSpec((k, tn), lambda j, i: (0, j))],
        out_specs=pl.BlockSpec((tm, tn), lambda j, i: (i, j)),
        out_shape=jax.ShapeDtypeStruct((m, n), out_dtype),
        compiler_params=_cparams(("parallel", "parallel")),
        name=name,
    )(a, w)


def _mm_deinterleave_kernel(a_ref, w_ref, o_ref, acc, *, dil):
    res = _dot(a_ref[...], w_ref[...])
    for c in range(acc.shape[0]):
        cs = slice(c * LANES, (c + 1) * LANES)
        acc[c] = res[:, cs]
        for r in range(dil):
            o_ref[r, :, cs] = acc[c, pl.ds(r, CHUNK, stride=dil), :].astype(o_ref.dtype)


def _matmul_deinterleave(a, w, dil, tn, name):
    m, k = a.shape
    n = w.shape[1]
    tm = CHUNK * dil
    return pl.pallas_call(
        functools.partial(_mm_deinterleave_kernel, dil=dil),
        grid=(n // tn, m // tm),
        in_specs=[pl.BlockSpec((tm, k), lambda j, i: (i, 0)),
                  pl.BlockSpec((k, tn), lambda j, i: (0, j))],
        out_specs=pl.BlockSpec((None, dil, CHUNK, tn), lambda j, i: (i, 0, 0, j)),
        out_shape=jax.ShapeDtypeStruct((m // tm, dil, CHUNK, n), BF16),
        scratch_shapes=[pltpu.VMEM((tn // LANES, tm, LANES), F32)],
        compiler_params=_cparams(("parallel", "parallel")),
        name=name,
    )(a, w)


def _ssd_kernel(z_ref, xs_ref, bm_ref, cm_ref, dt_ref,
                cwx_ref, cwb_ref, cwc_ref, cbx_ref, cbb_ref, cbc_ref,
                dtb_ref, alog_ref, dskip_ref, ng_ref,
                o_ref, xbuf, bbuf, cbuf, state, ybuf):
    c = pl.program_id(1)

    @pl.when(c == 0)
    def _():
        xbuf[0:HALO, :] = jnp.zeros((HALO, SSD_INNER), F32)
        bbuf[0:HALO, :] = jnp.zeros((HALO, SSD_BC), F32)
        cbuf[0:HALO, :] = jnp.zeros((HALO, SSD_BC), F32)
        state[...] = jnp.zeros_like(state)

    def conv_silu(in_ref, buf, w_ref, b_ref):
        buf[HALO:HALO + CHUNK, :] = in_ref[...].astype(F32)
        acc = b_ref[...] + w_ref[SSD_CONV - 1:SSD_CONV, :] * buf[HALO:HALO + CHUNK, :]
        for back in range(1, SSD_CONV):
            tap = SSD_CONV - 1 - back
            acc = acc + w_ref[tap:tap + 1, :] * buf[HALO - back:HALO - back + CHUNK, :]
        buf[0:HALO, :] = buf[CHUNK:CHUNK + HALO, :]
        return _silu(acc)

    xs = conv_silu(xs_ref, xbuf, cwx_ref, cbx_ref)
    bm = conv_silu(bm_ref, bbuf, cwb_ref, cbb_ref)
    cm = conv_silu(cm_ref, cbuf, cwc_ref, cbc_ref)
    xs_b = xs.astype(BF16)

    v = dt_ref[...] + dtb_ref[...]
    dt = jnp.maximum(v, 0.0) + jnp.log1p(jnp.exp(-jnp.abs(v)))
    a_neg = -jnp.exp(alog_ref[...])
    da = dt * a_neg
    row = lax.broadcasted_iota(jnp.int32, (CHUNK, CHUNK), 0)
    col = lax.broadcasted_iota(jnp.int32, (CHUNK, CHUNK), 1)
    causal = row >= col
    tril = jnp.where(causal, 1.0, 0.0).astype(F32)
    acs = jnp.dot(tril, da, preferred_element_type=F32, precision=lax.Precision.HIGHEST)
    acs_t = acs.T
    dt_t = dt.T
    total = acs[CHUNK - 1:CHUNK, :]
    w_t = (dt * jnp.exp(total - acs)).T

    for g in range(SSD_GROUPS):
        gs = slice(g * SSD_STATE, (g + 1) * SSD_STATE)
        bm_g = bm[:, gs]
        cm_g = cm[:, gs]
        bm_gb = bm_g.astype(BF16)
        cm_gb = cm_g.astype(BF16)
        cb = _dot_nt(cm_gb, bm_gb)
        bm_gt = bm_g.T
        for r in range(SSD_REP):
            h = g * SSD_REP + r
            hs = slice(h * SSD_HEAD_DIM, (h + 1) * SSD_HEAD_DIM)
            a_col = jnp.broadcast_to(acs[:, h:h + 1], (CHUNK, CHUNK))
            a_row = acs_t[h:h + 1, :]
            decay = jnp.exp(jnp.where(causal, a_col - a_row, -jnp.inf))
            m_h = (cb * decay * dt_t[h:h + 1, :]).astype(BF16)
            xs_h = xs_b[:, hs]
            y_diag = _dot(m_h, xs_h)
            e_col = jnp.exp(a_col)
            st_old = state[h]
            y_off = _dot(cm_gb, st_old.astype(BF16)) * e_col[:, :SSD_HEAD_DIM]
            st_new = _dot((bm_gt * w_t[h:h + 1, :]).astype(BF16), xs_h)
            state[h] = st_old * e_col[CHUNK - 1:CHUNK, :SSD_HEAD_DIM] + st_new
            ybuf[:, hs] = y_diag + y_off

    y = (ybuf[...] + xs * dskip_ref[...]) * _silu(z_ref[...].astype(F32))
    r = lax.rsqrt(jnp.mean(y * y, axis=-1, keepdims=True) + NORM_EPS)
    o_ref[...] = (y * r * ng_ref[...]).astype(o_ref.dtype)


def _ssd_branch(proj, dt_raw, conv_w, conv_b, dt_bias, a_log, d_skip, norm_g):
    b, s, _ = proj.shape
    nc = s // CHUNK

    def col(width, offset):
        return pl.BlockSpec((None, CHUNK, width), lambda bi, ci: (bi, ci, offset // width))

    def const(shape):
        return pl.BlockSpec(shape, lambda bi, ci: (0,) * len(shape))

    pad = LANES - SSD_HEADS
    cwx, cwb, cwc = (conv_w[:, :SSD_INNER], conv_w[:, SSD_INNER:SSD_INNER + SSD_BC],
                     conv_w[:, SSD_INNER + SSD_BC:])
    cbx, cbb, cbc = (conv_b[None, :SSD_INNER], conv_b[None, SSD_INNER:SSD_INNER + SSD_BC],
                     conv_b[None, SSD_INNER + SSD_BC:])
    dtb = jnp.pad(dt_bias, (0, pad))[None, :]
    alog = jnp.pad(a_log, (0, pad))[None, :]
    dskip = jnp.repeat(d_skip, SSD_HEAD_DIM)[None, :]
    return pl.pallas_call(
        _ssd_kernel,
        grid=(b, nc),
        in_specs=[col(SSD_INNER, COL_Z), col(SSD_INNER, COL_XS), col(SSD_BC, COL_B), col(SSD_BC, COL_C),
                  pl.BlockSpec((None, CHUNK, LANES), lambda bi, ci: (bi, ci, 0)),
                  const((SSD_CONV, SSD_INNER)), const((SSD_CONV, SSD_BC)), const((SSD_CONV, SSD_BC)),
                  const((1, SSD_INNER)), const((1, SSD_BC)), const((1, SSD_BC)),
                  const((1, LANES)), const((1, LANES)), const((1, SSD_INNER)), const((1, SSD_INNER))],
        out_specs=pl.BlockSpec((None, CHUNK, SSD_INNER), lambda bi, ci: (bi, ci, 0)),
        out_shape=jax.ShapeDtypeStruct((b, s, SSD_INNER), BF16),
        scratch_shapes=[pltpu.VMEM((CHUNK + HALO, SSD_INNER), F32),
                        pltpu.VMEM((CHUNK + HALO, SSD_BC), F32),
                        pltpu.VMEM((CHUNK + HALO, SSD_BC), F32),
                        pltpu.VMEM((SSD_HEADS, SSD_STATE, SSD_HEAD_DIM), F32),
                        pltpu.VMEM((CHUNK, SSD_INNER), F32)],
        compiler_params=_cparams(("parallel", "arbitrary")),
        name="ssd_mixer",
    )(proj, proj, proj, proj, dt_raw, cwx, cwb, cwc, cbx, cbb, cbc, dtb, alog, dskip, norm_g[None, :])


def _rope(t, cosf, sinf):
    return t * cosf + pltpu.roll(t, DIL_HEAD_DIM // 2, 1) * sinf


def _rope_tables(s):
    inv_freq = ROPE_THETA ** (-jnp.arange(0, RET_QK_DIM, 2, dtype=F32) / RET_QK_DIM)
    ang = jnp.arange(s, dtype=F32)[:, None] * inv_freq[None, :]
    cos, sin = jnp.cos(ang), jnp.sin(ang)
    return jnp.concatenate([cos, cos], axis=-1), jnp.concatenate([-sin, sin], axis=-1)


_LOG_GAMMA = [float(np.log(np.float32(1.0) - np.exp2(np.float32(-5.0 - h)))) for h in range(RET_HEADS)]


def _ret_kernel(q_ref, k_ref, v_ref, g_ref, cos_ref, sin_ref, o_ref, state, dmask, qdec, kdec):
    c = pl.program_id(1)

    @pl.when(c == 0)
    def _():
        state[...] = jnp.zeros_like(state)
        li = lax.broadcasted_iota(jnp.int32, (CHUNK, CHUNK), 0).astype(F32)
        si = lax.broadcasted_iota(jnp.int32, (CHUNK, CHUNK), 1).astype(F32)
        rel = li - si
        for h in range(RET_HEADS):
            lg = _LOG_GAMMA[h]
            dmask[h] = jnp.where(rel >= 0, jnp.exp(jnp.maximum(rel, 0.0) * lg), 0.0)
            qdec[h] = jnp.exp((li + 1.0) * lg)
            kdec[h] = jnp.exp((CHUNK - 1.0 - li) * lg)

    cosf = cos_ref[...]
    sinf = sin_ref[...]
    for h in range(RET_HEADS):
        qs = slice(h * RET_QK_DIM, (h + 1) * RET_QK_DIM)
        vs = slice(h * RET_V_DIM, (h + 1) * RET_V_DIM)
        qr = _rope(q_ref[:, qs].astype(F32), cosf, sinf)
        kr = _rope(k_ref[:, qs].astype(F32), cosf, sinf) * (RET_QK_DIM ** -0.5)
        v_h = v_ref[:, vs]
        scores = _dot_nt(qr.astype(BF16), kr.astype(BF16)) * dmask[h]
        inner = _dot(scores.astype(BF16), v_h)
        st_old = state[h]
        cross = _dot((qr * qdec[h]).astype(BF16), st_old.astype(BF16))
        kv = _dot((kr * kdec[h]).T.astype(BF16), v_h)
        state[h] = st_old * math.exp(CHUNK * _LOG_GAMMA[h]) + kv
        o = inner + cross
        o = o * lax.rsqrt(jnp.mean(o * o, axis=-1, keepdims=True) + NORM_EPS)
        o_ref[:, vs] = (_silu(g_ref[:, vs].astype(F32)) * o).astype(o_ref.dtype)


def _ret_branch(proj, cosf, sinf):
    b, s, _ = proj.shape
    nc = s // CHUNK

    def col(width, offset):
        return pl.BlockSpec((None, CHUNK, width), lambda bi, ci: (bi, ci, offset // width))

    tab = pl.BlockSpec((CHUNK, RET_QK_DIM), lambda bi, ci: (ci, 0))
    return pl.pallas_call(
        _ret_kernel,
        grid=(b, nc),
        in_specs=[col(RET_QK, COL_RQ), col(RET_QK, COL_RK), col(RET_V, COL_RV), col(RET_V, COL_RG), tab, tab],
        out_specs=pl.BlockSpec((None, CHUNK, RET_V), lambda bi, ci: (bi, ci, 0)),
        out_shape=jax.ShapeDtypeStruct((b, s, RET_V), BF16),
        scratch_shapes=[pltpu.VMEM((RET_HEADS, RET_QK_DIM, RET_V_DIM), F32),
                        pltpu.VMEM((RET_HEADS, CHUNK, CHUNK), F32),
                        pltpu.VMEM((RET_HEADS, CHUNK, CHUNK), F32),
                        pltpu.VMEM((RET_HEADS, CHUNK, CHUNK), F32)],
        compiler_params=_cparams(("parallel", "arbitrary")),
        name="retention_mixer",
    )(proj, proj, proj, proj, cosf, sinf)


def _dil_kernel(q_ref, k_ref, v_ref, cos_ref, sin_ref, o_ref, lse_ref, kprev, vprev):
    n = pl.program_id(2)

    @pl.when(n == 0)
    def _():
        kprev[...] = jnp.zeros_like(kprev)
        vprev[...] = jnp.zeros_like(vprev)

    cosf = cos_ref[...]
    sinf = sin_ref[...]
    scale = DIL_HEAD_DIM ** -0.5
    cosq = cosf * scale
    sinq = sinf * scale
    qi = lax.broadcasted_iota(jnp.int32, (CHUNK, CHUNK), 0)
    kj = lax.broadcasted_iota(jnp.int32, (CHUNK, CHUNK), 1)
    mask_cur = kj <= qi
    mask_prev = jnp.logical_and(kj >= qi, n > 0)
    ones = jnp.ones((CHUNK, LANES), BF16)
    lse_all = jnp.zeros((CHUNK, LANES), F32)
    for h0 in range(0, DIL_HEADS, DIL_HEAD_GROUP):
        heads = range(h0, h0 + DIL_HEAD_GROUP)
        sl = {h: slice(h * DIL_HEAD_DIM, (h + 1) * DIL_HEAD_DIM) for h in heads}
        qb = {h: _rope(q_ref[:, sl[h]].astype(F32), cosq, sinq).astype(BF16) for h in heads}
        kb = {h: _rope(k_ref[:, sl[h]].astype(F32), cosf, sinf).astype(BF16) for h in heads}
        s_cur = {h: jnp.where(mask_cur, _dot_nt(qb[h], kb[h]), -jnp.inf) for h in heads}
        s_prev = {h: jnp.where(mask_prev, _dot_nt(qb[h], kprev[:, sl[h]]), -jnp.inf) for h in heads}
        mx = {h: jnp.max(jnp.maximum(s_cur[h], s_prev[h]), axis=-1, keepdims=True) for h in heads}
        p_cur = {h: jnp.exp(s_cur[h] - mx[h]).astype(BF16) for h in heads}
        p_prev = {h: jnp.exp(s_prev[h] - mx[h]).astype(BF16) for h in heads}
        acc = {h: _dot(p_cur[h], v_ref[:, sl[h]]) + _dot(p_prev[h], vprev[:, sl[h]]) for h in heads}
        den = {h: _dot(p_cur[h], ones) + _dot(p_prev[h], ones) for h in heads}
        for h in heads:
            o_ref[:, sl[h]] = (acc[h] / den[h]).astype(o_ref.dtype)
            lse_all = jnp.where(kj == h, mx[h] + jnp.log(den[h]), lse_all)
            kprev[:, sl[h]] = kb[h]
    vprev[...] = v_ref[...]
    lse_ref[...] = lse_all


def _dil_group(qkv, col0, dil, batch, cos_t, sin_t):
    tiles = qkv.shape[0]
    nb = tiles // batch
    cb = col0 // DIL_WIDTH

    def part(which):
        return pl.BlockSpec((None, None, CHUNK, DIL_WIDTH), lambda bi, r, n: (bi * nb + n, r, 0, cb + which))

    tab = pl.BlockSpec((None, None, CHUNK, DIL_HEAD_DIM), lambda bi, r, n: (n, r, 0, 0))
    return pl.pallas_call(
        _dil_kernel,
        grid=(batch, dil, nb),
        in_specs=[part(0), part(1), part(2), tab, tab],
        out_specs=[pl.BlockSpec((None, None, CHUNK, DIL_WIDTH), lambda bi, r, n: (bi * nb + n, r, 0, 0)),
                   pl.BlockSpec((None, None, CHUNK, LANES), lambda bi, r, n: (bi * nb + n, r, 0, 0))],
        out_shape=[jax.ShapeDtypeStruct((tiles, dil, CHUNK, DIL_WIDTH), BF16),
                   jax.ShapeDtypeStruct((tiles, dil, CHUNK, LANES), F32)],
        scratch_shapes=[pltpu.VMEM((CHUNK, DIL_WIDTH), BF16), pltpu.VMEM((CHUNK, DIL_WIDTH), BF16)],
        compiler_params=_cparams(("parallel", "parallel", "arbitrary")),
        name=f"dilated_attention_d{dil}",
    )(qkv, qkv, qkv, cos_t, sin_t)


def _dil_combine_kernel(o0_ref, o1_ref, o2_ref, l0_ref, l1_ref, l2_ref, g_ref, y_ref, ob1, ob2, lb1, lb2):
    for (o_ref, l_ref, ob, lb, dil) in ((o1_ref, l1_ref, ob1, lb1, DIL_DILATIONS[1]),
                                        (o2_ref, l2_ref, ob2, lb2, DIL_DILATIONS[2])):
        rows = o_ref.shape[1]
        for r in range(dil):
            lb[pl.ds(r, rows, stride=dil), :] = l_ref[r]
            for h in range(DIL_HEADS):
                hs = slice(h * DIL_HEAD_DIM, (h + 1) * DIL_HEAD_DIM)
                ob[h, pl.ds(r, rows, stride=dil), :] = o_ref[r, :, hs].astype(F32)

    l0, l1, l2 = l0_ref[...], lb1[...], lb2[...]
    m = jnp.maximum(jnp.maximum(l0, l1), l2)
    e0, e1, e2 = jnp.exp(l0 - m), jnp.exp(l1 - m), jnp.exp(l2 - m)
    inv = 1.0 / (e0 + e1 + e2)
    w0, w1, w2 = e0 * inv, e1 * inv, e2 * inv
    rows = l0.shape[0]
    for h in range(DIL_HEADS):
        hs = slice(h * DIL_HEAD_DIM, (h + 1) * DIL_HEAD_DIM)

        def lane(w):
            return jnp.broadcast_to(w[:, h:h + 1], (rows, DIL_HEAD_DIM))

        o = lane(w0) * o0_ref[:, hs].astype(F32) + lane(w1) * ob1[h] + lane(w2) * ob2[h]
        y_ref[:, hs] = (_silu(g_ref[:, hs].astype(F32)) * o).astype(y_ref.dtype)


def _dil_combine(outs, lses, proj2d):
    m = proj2d.shape[0]
    d1, d2 = DIL_DILATIONS[1], DIL_DILATIONS[2]
    tm = CHUNK * d1
    sub = tm // d2
    per = (CHUNK * d2) // tm
    o_spec = pl.BlockSpec((tm, DIL_WIDTH), lambda i: (i, 0))
    l_spec = pl.BlockSpec((tm, LANES), lambda i: (i, 0))

    def grouped(width):
        return (pl.BlockSpec((None, d1, CHUNK, width), lambda i: (i, 0, 0, 0)),
                pl.BlockSpec((None, d2, sub, width), lambda i: (i // per, 0, i % per, 0)))

    o1_spec, o2_spec = grouped(DIL_WIDTH)
    l1_spec, l2_spec = grouped(LANES)
    return pl.pallas_call(
        _dil_combine_kernel,
        grid=(m // tm,),
        in_specs=[o_spec, o1_spec, o2_spec, l_spec, l1_spec, l2_spec,
                  pl.BlockSpec((tm, DIL_WIDTH), lambda i: (i, COL_DG // DIL_WIDTH))],
        out_specs=o_spec,
        out_shape=jax.ShapeDtypeStruct((m, DIL_WIDTH), BF16),
        scratch_shapes=[pltpu.VMEM((DIL_HEADS, tm, DIL_HEAD_DIM), F32), pltpu.VMEM((DIL_HEADS, tm, DIL_HEAD_DIM), F32),
                        pltpu.VMEM((tm, LANES), F32), pltpu.VMEM((tm, LANES), F32)],
        compiler_params=_cparams(("parallel",)),
        name="dilated_combine",
    )(outs[0].reshape(m, DIL_WIDTH), outs[1], outs[2], lses[0].reshape(m, LANES), lses[1], lses[2], proj2d)


def _merge_kernel(ya_ref, yb_ref, yc_ref, wa_ref, wb_ref, wc_ref, ga_ref, gb_ref, gc_ref, o_ref):
    acc = _sigmoid(ga_ref[...].astype(F32)) * _dot(ya_ref[...], wa_ref[...])
    acc = acc + _sigmoid(gb_ref[...].astype(F32)) * _dot(yb_ref[...], wb_ref[...])
    acc = acc + _sigmoid(gc_ref[...].astype(F32)) * _dot(yc_ref[...], wc_ref[...])
    o_ref[...] = acc.astype(o_ref.dtype)


def _merge(ya, yb, yc, wa, wb, wc, proj2d, tm=512, tn=1024):
    m = ya.shape[0]
    d = wa.shape[1]

    def act(width):
        return pl.BlockSpec((tm, width), lambda j, i: (i, 0))

    def wgt(width):
        return pl.BlockSpec((width, tn), lambda j, i: (0, j))

    def gate(branch):
        return pl.BlockSpec((tm, tn), lambda j, i: (i, (COL_MG + branch * d) // tn + j))

    return pl.pallas_call(
        _merge_kernel,
        grid=(d // tn, m // tm),
        in_specs=[act(SSD_INNER), act(RET_V), act(DIL_WIDTH), wgt(SSD_INNER), wgt(RET_V), wgt(DIL_WIDTH),
                  gate(0), gate(1), gate(2)],
        out_specs=pl.BlockSpec((tm, tn), lambda j, i: (i, j)),
        out_shape=jax.ShapeDtypeStruct((m, d), BF16),
        compiler_params=_cparams(("parallel", "parallel")),
        name="gated_merge",
    )(ya, yb, yc, wa, wb, wc, proj2d, proj2d, proj2d)


def _out_kernel(a_ref, w_ref, x_ref, g_ref, xo_ref, ho_ref):
    x = x_ref[...] + _dot(a_ref[...], w_ref[...])
    xo_ref[...] = x
    r = lax.rsqrt(jnp.mean(x * x, axis=-1, keepdims=True) + NORM_EPS)
    ho_ref[...] = (x * r * g_ref[...]).astype(ho_ref.dtype)


def _out_proj(merged, w_out, x2d, g_next, norm_dtype, tm=512):
    m, d = x2d.shape
    row = pl.BlockSpec((tm, d), lambda i: (i, 0))
    return pl.pallas_call(
        _out_kernel,
        grid=(m // tm,),
        in_specs=[row, pl.BlockSpec((d, d), lambda i: (0, 0)), row, pl.BlockSpec((1, d), lambda i: (0, 0))],
        out_specs=[row, row],
        out_shape=[jax.ShapeDtypeStruct((m, d), F32), jax.ShapeDtypeStruct((m, d), norm_dtype)],
        compiler_params=_cparams(("parallel",)),
        name="out_proj_residual_norm",
    )(merged, w_out, x2d, g_next.reshape(1, d))


def _split_w_in(w):
    z, xs, bmat, cmat = w[:, 0:2048], w[:, 2048:4096], w[:, 4096:4608], w[:, 4608:5120]
    dt = w[:, 5120:5152]
    rq, rk, rv, rg = w[:, 5152:6176], w[:, 6176:7200], w[:, 7200:9248], w[:, 9248:11296]
    dq0, dq1, dq2 = w[:, 11296:14368], w[:, 14368:17440], w[:, 17440:20512]
    dg, mg = w[:, 20512:21536], w[:, 21536:27680]
    main = jnp.concatenate([z, xs, rv, rg, mg, dq0, rq, rk, dg, bmat, cmat], axis=1).astype(BF16)
    dt = jnp.pad(dt, ((0, 0), (0, LANES - SSD_HEADS))).astype(BF16)
    return main, dt, dq1.astype(BF16), dq2.astype(BF16)


def kernel(x, norm_g, w_in, conv_w, conv_b, dt_bias, a_log, d_skip, ssd_norm_g,
           w_o_ssd, w_o_ret, w_o_dil, w_out, final_norm_g):
    b, s, d = x.shape
    depth = w_in.shape[0]
    m = b * s
    assert s % (max(DIL_DILATIONS) * CHUNK) == 0 and d == 2048
    cosf, sinf = _rope_tables(s)
    tabs = []
    for dil in DIL_DILATIONS:
        def grouped(t):
            return t.reshape(s // (CHUNK * dil), CHUNK, dil, DIL_HEAD_DIM).transpose(0, 2, 1, 3)
        tabs.append((grouped(cosf), grouped(sinf)))

    x2d = x.reshape(m, d)
    h = _rms_norm(x2d, norm_g[0], BF16)
    for layer in range(depth):
        w_main, w_dt, w_dq1, w_dq2 = _split_w_in(w_in[layer])
        proj2d = _matmul(h, w_main, BF16, 1024, 1024, "in_proj")
        dt_raw = _matmul(h, w_dt, F32, 1024, LANES, "dt_proj")
        proj = proj2d.reshape(b, s, N_MAIN)

        y_a = _ssd_branch(proj, dt_raw.reshape(b, s, LANES), conv_w[layer], conv_b[layer], dt_bias[layer],
                          a_log[layer], d_skip[layer], ssd_norm_g[layer])
        y_b = _ret_branch(proj, cosf, sinf)

        outs, lses = [], []
        for gi, dil in enumerate(DIL_DILATIONS):
            if dil == 1:
                qkv, col0 = proj2d.reshape(m // CHUNK, 1, CHUNK, N_MAIN), COL_DQKV0
            else:
                w_g = w_dq1 if gi == 1 else w_dq2
                qkv, col0 = _matmul_deinterleave(h, w_g, dil, 1024, f"dqkv_proj_d{dil}"), 0
            o, lse = _dil_group(qkv, col0, dil, b, tabs[gi][0], tabs[gi][1])
            outs.append(o)
            lses.append(lse)
        y_c = _dil_combine(outs, lses, proj2d)

        merged = _merge(y_a.reshape(m, SSD_INNER), y_b.reshape(m, RET_V), y_c,
                        w_o_ssd[layer].astype(BF16), w_o_ret[layer].astype(BF16), w_o_dil[layer].astype(BF16),
                        proj2d)
        last = layer == depth - 1
        g_next = final_norm_g if last else norm_g[layer + 1]
        x2d, h = _out_proj(merged, w_out[layer].astype(BF16), x2d, g_next, F32 if last else BF16)
    return h.reshape(b, s, d)
```

```python
import functools
import math

import numpy as np
import jax
import jax.numpy as jnp
from jax import lax
from jax.experimental import pallas as pl
from jax.experimental.pallas import tpu as pltpu

F32 = jnp.float32
BF16 = jnp.bfloat16

NORM_EPS = 1e-6
CHUNK = 128
ROPE_THETA = 10000.0
LOG2E = 1.4426950408889634

SSD_INNER = 2048
SSD_HEAD_DIM = 64
SSD_HEADS = 32
SSD_GROUPS = 4
SSD_REP = SSD_HEADS // SSD_GROUPS
SSD_STATE = 128
SSD_CONV = 4
SSD_BC = SSD_GROUPS * SSD_STATE

RET_HEADS = 8
RET_QK_DIM = 128
RET_V_DIM = 256
RET_QK = RET_HEADS * RET_QK_DIM
RET_V = RET_HEADS * RET_V_DIM

DIL_DILATIONS = (1, 4, 16)
DIL_HEADS = 8
DIL_HEAD_DIM = 128
DIL_WIDTH = DIL_HEADS * DIL_HEAD_DIM
DIL_HEAD_GROUP = 4
QK_SCALE = DIL_HEAD_DIM ** -0.5

LANES = 128
VMEM_LIMIT = 56 * 1024 * 1024
PROJ_TILE = 1024

T_Z, T_XS, T_RV, T_RG, T_MG = 0, 2, 4, 6, 8
T_DQ, T_RK, T_DK, T_RQ, T_DV, T_DG, T_BC = 14, 15, 16, 17, 18, 19, 20
ROPE_LO, ROPE_HI, ROPE_SCALED = 14, 18, 2
N_MAIN = 21 * PROJ_TILE


def _cparams(sem):
    return pltpu.CompilerParams(dimension_semantics=sem, vmem_limit_bytes=VMEM_LIMIT)


def _sigmoid(v):
    return 1.0 / (1.0 + jnp.exp(-v))


def _silu(v):
    return v * _sigmoid(v)


def _dot(a, b):
    return jnp.dot(a, b, preferred_element_type=F32)


def _dot_nt(a, b):
    return lax.dot_general(a, b, (((1,), (1,)), ((), ())), preferred_element_type=F32)


def _norm_kernel(x_ref, g_ref, o_ref):
    x = x_ref[...]
    r = lax.rsqrt(jnp.mean(x * x, axis=-1, keepdims=True) + NORM_EPS)
    o_ref[...] = (x * r * g_ref[...]).astype(o_ref.dtype)


def _rms_norm(x2d, g, out_dtype, tm=512):
    m, d = x2d.shape
    return pl.pallas_call(
        _norm_kernel,
        grid=(m // tm,),
        in_specs=[pl.BlockSpec((tm, d), lambda i: (i, 0)),
                  pl.BlockSpec((1, d), lambda i: (0, 0))],
        out_specs=pl.BlockSpec((tm, d), lambda i: (i, 0)),
        out_shape=jax.ShapeDtypeStruct((m, d), out_dtype),
        compiler_params=_cparams(("parallel",)),
        name="rms_norm",
    )(x2d, g.reshape(1, d))


def _mm_kernel(a_ref, w_ref, o_ref):
    o_ref[...] = _dot(a_ref[...], w_ref[...]).astype(o_ref.dtype)


def _matmul(a, w, out_dtype, tm, tn, name):
    m, k = a.shape
    n = w.shape[1]
    return pl.pallas_call(
        _mm_kernel,
        grid=(n // tn, m // tm),
        in_specs=[pl.BlockSpec((tm, k), lambda j, i: (i, 0)),
                  pl.BlockSpec((k, tn), lambda j, i: (0, j))],
        out_specs=pl.BlockSpec((tm, tn), lambda j, i: (i, j)),
        out_shape=jax.ShapeDtypeStruct((m, n), out_dtype),
        compiler_params=_cparams(("parallel", "parallel")),
        name=name,
    )(a, w)


def _proj_kernel(a_ref, w_ref, cos_ref, sin_ref, o_ref, *scratch, rope_lo, rope_hi, n_scaled, dil):
    j = pl.program_id(0)

    def body(rope):
        res = _dot(a_ref[...], w_ref[...])
        if rope:
            factor = jnp.where(j < rope_lo + n_scaled, QK_SCALE, 1.0)
            cosf = cos_ref[...] * factor
            sinf = sin_ref[...] * factor
        for c in range(res.shape[1] // LANES):
            cs = slice(c * LANES, (c + 1) * LANES)
            blk = res[:, cs]
            if rope:
                blk = blk * cosf + pltpu.roll(blk, DIL_HEAD_DIM // 2, 1) * sinf
            if dil == 1:
                o_ref[:, cs] = blk.astype(o_ref.dtype)
            else:
                acc = scratch[0]
                acc[c] = blk
                for r in range(dil):
                    o_ref[r, :, cs] = acc[c, pl.ds(r, CHUNK, stride=dil), :].astype(o_ref.dtype)

    is_rope = jnp.logical_and(j >= rope_lo, j < rope_hi)
    pl.when(is_rope)(functools.partial(body, True))
    pl.when(jnp.logical_not(is_rope))(functools.partial(body, False))


def _projection(a, w, cosf, sinf, *, rope_lo, rope_hi, n_scaled, dil, name):
    m, k = a.shape
    n = w.shape[1]
    s = cosf.shape[0]
    tn = PROJ_TILE
    tm = PROJ_TILE if dil == 1 else CHUNK * dil
    tab = pl.BlockSpec((tm, DIL_HEAD_DIM), lambda j, i: (i % (s // tm), 0))
    if dil == 1:
        out_spec = pl.BlockSpec((tm, tn), lambda j, i: (i, j))
        out_shape = jax.ShapeDtypeStruct((m, n), BF16)
        scratch = []
    else:
        out_spec = pl.BlockSpec((None, dil, CHUNK, tn), lambda j, i: (i, 0, 0, j))
        out_shape = jax.ShapeDtypeStruct((m // tm, dil, CHUNK, n), BF16)
        scratch = [pltpu.VMEM((tn // LANES, tm, LANES), F32)]
    return pl.pallas_call(
        functools.partial(_proj_kernel, rope_lo=rope_lo, rope_hi=rope_hi, n_scaled=n_scaled, dil=dil),
        grid=(n // tn, m // tm),
        in_specs=[pl.BlockSpec((tm, k), lambda j, i: (i, 0)),
                  pl.BlockSpec((k, tn), lambda j, i: (0, j)), tab, tab],
        out_specs=out_spec,
        out_shape=out_shape,
        scratch_shapes=scratch,
        compiler_params=_cparams(("parallel", "parallel")),
        name=name,
    )(a, w, cosf, sinf)


def _ssd_kernel(z_ref, xs_ref, bc_ref, dt_ref, cwx_ref, cwbc_ref, cbx_ref, cbbc_ref,
                dtb_ref, alog_ref, dskip_ref, ng_ref, o_ref, xbuf, bcbuf, state, ybuf):
    c = pl.program_id(1)

    @pl.when(c == 0)
    def _():
        xbuf[CHUNK:, :] = jnp.zeros((CHUNK, SSD_INNER), BF16)
        bcbuf[CHUNK:, :] = jnp.zeros((CHUNK, 2 * SSD_BC), BF16)
        state[...] = jnp.zeros_like(state)

    srow = lax.broadcasted_iota(jnp.int32, (CHUNK, 2 * CHUNK), 0)
    scol = lax.broadcasted_iota(jnp.int32, (CHUNK, 2 * CHUNK), 1)
    shifts = [jnp.where(scol == ((srow - back) & (2 * CHUNK - 1)), 1.0, 0.0).astype(BF16)
              for back in range(1, SSD_CONV)]

    def conv_silu(in_ref, buf, w_ref, b_ref):
        cur = in_ref[...]
        buf[0:CHUNK, :] = cur
        ext = buf[...]
        acc = b_ref[...] + w_ref[SSD_CONV - 1:SSD_CONV, :] * cur.astype(F32)
        for back in range(1, SSD_CONV):
            tap = SSD_CONV - 1 - back
            acc = acc + w_ref[tap:tap + 1, :] * _dot(shifts[back - 1], ext)
        buf[CHUNK:, :] = cur
        return _silu(acc)

    xs = conv_silu(xs_ref, xbuf, cwx_ref, cbx_ref)
    bc = conv_silu(bc_ref, bcbuf, cwbc_ref, cbbc_ref)
    xs_b = xs.astype(BF16)

    v = dt_ref[...] + dtb_ref[...]
    dt = jnp.maximum(v, 0.0) + jnp.log1p(jnp.exp(-jnp.abs(v)))
    a_neg = -jnp.exp(alog_ref[...])
    da = dt * a_neg
    row = lax.broadcasted_iota(jnp.int32, (CHUNK, CHUNK), 0)
    col = lax.broadcasted_iota(jnp.int32, (CHUNK, CHUNK), 1)
    causal = row >= col
    tril = jnp.where(causal, 1.0, 0.0).astype(F32)
    acs = jnp.dot(tril, da, preferred_element_type=F32, precision=lax.Precision.HIGHEST)
    acs2 = acs * LOG2E
    ldt = jnp.log2(dt)
    arow_t = (acs2 - ldt).T
    total2 = acs2[CHUNK - 1:CHUNK, :]
    w_t = jnp.exp2(total2 - acs2 + ldt).T
    lane_lo = lax.broadcasted_iota(jnp.int32, (CHUNK, LANES), 1) < SSD_HEAD_DIM

    for g in range(SSD_GROUPS):
        bm_g = bc[:, g * SSD_STATE:(g + 1) * SSD_STATE]
        cm_g = bc[:, SSD_BC + g * SSD_STATE:SSD_BC + (g + 1) * SSD_STATE]
        cb = _dot_nt(cm_g.astype(BF16), bm_g.astype(BF16))
        bm_gt = bm_g.T
        for pair in range(SSD_REP // 2):
            h0 = g * SSD_REP + 2 * pair
            ps = slice(h0 * SSD_HEAD_DIM, (h0 + 2) * SSD_HEAD_DIM)
            xs_p = xs_b[:, ps]
            st_old = state[h0 // 2]
            rhs = jnp.concatenate([xs_p, st_old.astype(BF16)], axis=0)
            ys, sts, cds = [], [], []
            for h in (h0, h0 + 1):
                a_col = jnp.broadcast_to(acs2[:, h:h + 1], (CHUNK, CHUNK))
                decay = jnp.exp2(jnp.where(causal, a_col - arow_t[h:h + 1, :], -jnp.inf))
                e_col = jnp.exp2(a_col)
                lhs = jnp.concatenate([(cb * decay).astype(BF16), (cm_g * e_col).astype(BF16)], axis=1)
                ys.append(_dot(lhs, rhs))
                sts.append(_dot((bm_gt * w_t[h:h + 1, :]).astype(BF16), xs_p))
                cds.append(e_col[CHUNK - 1:CHUNK, :])
            ybuf[:, ps] = jnp.where(lane_lo, ys[0], ys[1])
            chunk_decay = jnp.where(lane_lo[0:1, :], cds[0], cds[1])
            state[h0 // 2] = st_old * chunk_decay + jnp.where(lane_lo, sts[0], sts[1])

    y = (ybuf[...] + xs * dskip_ref[...]) * _silu(z_ref[...].astype(F32))
    r = lax.rsqrt(jnp.mean(y * y, axis=-1, keepdims=True) + NORM_EPS)
    o_ref[...] = (y * r * ng_ref[...]).astype(o_ref.dtype)


def _ssd_branch(proj, dt_raw, conv_w, conv_b, dt_bias, a_log, d_skip, norm_g):
    b, s, _ = proj.shape
    nc = s // CHUNK

    def col(width, tile):
        return pl.BlockSpec((None, CHUNK, width), lambda bi, ci: (bi, ci, tile * PROJ_TILE // width))

    def const(shape):
        return pl.BlockSpec(shape, lambda bi, ci: (0,) * len(shape))

    pad = LANES - SSD_HEADS
    dtb = jnp.pad(dt_bias, (0, pad))[None, :]
    alog = jnp.pad(a_log, (0, pad))[None, :]
    dskip = jnp.repeat(d_skip, SSD_HEAD_DIM)[None, :]
    return pl.pallas_call(
        _ssd_kernel,
        grid=(b, nc),
        in_specs=[col(SSD_INNER, T_Z), col(SSD_INNER, T_XS), col(2 * SSD_BC, T_BC),
                  pl.BlockSpec((None, CHUNK, LANES), lambda bi, ci: (bi, ci, 0)),
                  const((SSD_CONV, SSD_INNER)), const((SSD_CONV, 2 * SSD_BC)),
                  const((1, SSD_INNER)), const((1, 2 * SSD_BC)),
                  const((1, LANES)), const((1, LANES)), const((1, SSD_INNER)), const((1, SSD_INNER))],
        out_specs=pl.BlockSpec((None, CHUNK, SSD_INNER), lambda bi, ci: (bi, ci, 0)),
        out_shape=jax.ShapeDtypeStruct((b, s, SSD_INNER), BF16),
        scratch_shapes=[pltpu.VMEM((2 * CHUNK, SSD_INNER), BF16),
                        pltpu.VMEM((2 * CHUNK, 2 * SSD_BC), BF16),
                        pltpu.VMEM((SSD_HEADS // 2, SSD_STATE, 2 * SSD_HEAD_DIM), F32),
                        pltpu.VMEM((CHUNK, SSD_INNER), F32)],
        compiler_params=_cparams(("parallel", "arbitrary")),
        name="ssd_mixer",
    )(proj, proj, proj, dt_raw, conv_w[:, :SSD_INNER], conv_w[:, SSD_INNER:],
      conv_b[None, :SSD_INNER], conv_b[None, SSD_INNER:], dtb, alog, dskip, norm_g[None, :])


def _rope_tables(s):
    inv_freq = ROPE_THETA ** (-jnp.arange(0, RET_QK_DIM, 2, dtype=F32) / RET_QK_DIM)
    ang = jnp.arange(s, dtype=F32)[:, None] * inv_freq[None, :]
    cos, sin = jnp.cos(ang), jnp.sin(ang)
    return jnp.concatenate([cos, cos], axis=-1), jnp.concatenate([-sin, sin], axis=-1)


_LOG_GAMMA = [float(np.log(np.float32(1.0) - np.exp2(np.float32(-5.0 - h)))) for h in range(RET_HEADS)]


def _ret_kernel(q_ref, k_ref, v_ref, g_ref, o_ref, state, dmask, qdec, kdec):
    c = pl.program_id(1)

    @pl.when(c == 0)
    def _():
        state[...] = jnp.zeros_like(state)
        li = lax.broadcasted_iota(jnp.int32, (CHUNK, CHUNK), 0).astype(F32)
        si = lax.broadcasted_iota(jnp.int32, (CHUNK, CHUNK), 1).astype(F32)
        rel = li - si
        for h in range(RET_HEADS):
            lg = _LOG_GAMMA[h]
            dmask[h] = jnp.where(rel >= 0, jnp.exp(jnp.maximum(rel, 0.0) * lg), 0.0)
            qdec[h] = jnp.exp((li + 1.0) * lg)
            kdec[h] = jnp.exp((CHUNK - 1.0 - li) * lg)

    for h in range(RET_HEADS):
        qs = slice(h * RET_QK_DIM, (h + 1) * RET_QK_DIM)
        vs = slice(h * RET_V_DIM, (h + 1) * RET_V_DIM)
        qr = q_ref[:, qs]
        kr = k_ref[:, qs]
        v_h = v_ref[:, vs]
        scores = _dot_nt(qr, kr) * dmask[h]
        inner = _dot(scores.astype(BF16), v_h)
        st_old = state[h]
        cross = _dot((qr.astype(F32) * qdec[h]).astype(BF16), st_old.astype(BF16))
        kv = _dot((kr.astype(F32) * kdec[h]).T.astype(BF16), v_h)
        state[h] = st_old * math.exp(CHUNK * _LOG_GAMMA[h]) + kv
        o = inner + cross
        o = o * lax.rsqrt(jnp.mean(o * o, axis=-1, keepdims=True) + NORM_EPS)
        o_ref[:, vs] = (_silu(g_ref[:, vs].astype(F32)) * o).astype(o_ref.dtype)


def _ret_branch(proj):
    b, s, _ = proj.shape
    nc = s // CHUNK

    def col(width, tile):
        return pl.BlockSpec((None, CHUNK, width), lambda bi, ci: (bi, ci, tile * PROJ_TILE // width))

    return pl.pallas_call(
        _ret_kernel,
        grid=(b, nc),
        in_specs=[col(RET_QK, T_RQ), col(RET_QK, T_RK), col(RET_V, T_RV), col(RET_V, T_RG)],
        out_specs=pl.BlockSpec((None, CHUNK, RET_V), lambda bi, ci: (bi, ci, 0)),
        out_shape=jax.ShapeDtypeStruct((b, s, RET_V), BF16),
        scratch_shapes=[pltpu.VMEM((RET_HEADS, RET_QK_DIM, RET_V_DIM), F32),
                        pltpu.VMEM((RET_HEADS, CHUNK, CHUNK), F32),
                        pltpu.VMEM((RET_HEADS, CHUNK, CHUNK), F32),
                        pltpu.VMEM((RET_HEADS, CHUNK, CHUNK), F32)],
        compiler_params=_cparams(("parallel", "arbitrary")),
        name="retention_mixer",
    )(proj, proj, proj, proj)


def _dil_kernel(q_ref, k_ref, v_ref, o_ref, lse_ref, kbuf, vbuf):
    n = pl.program_id(2)
    ext = 2 * DIL_HEAD_DIM

    @pl.when(n == 0)
    def _():
        kbuf[...] = jnp.zeros_like(kbuf)
        for h in range(DIL_HEADS):
            vbuf[:, h * ext:h * ext + DIL_HEAD_DIM] = jnp.zeros((2 * CHUNK, DIL_HEAD_DIM), BF16)
            vbuf[:, h * ext + DIL_HEAD_DIM:(h + 1) * ext] = jnp.ones((2 * CHUNK, DIL_HEAD_DIM), BF16)

    parity = n % 2
    off = pl.multiple_of(parity * CHUNK, CHUNK)
    kbuf[pl.ds(off, CHUNK), :] = k_ref[...]
    for h in range(DIL_HEADS):
        vbuf[pl.ds(off, CHUNK), h * ext:h * ext + DIL_HEAD_DIM] = v_ref[:, h * DIL_HEAD_DIM:(h + 1) * DIL_HEAD_DIM]

    qi = lax.broadcasted_iota(jnp.int32, (CHUNK, 2 * CHUNK), 0)
    kj = lax.broadcasted_iota(jnp.int32, (CHUNK, 2 * CHUNK), 1)
    kl = kj & (CHUNK - 1)
    is_cur = (kj >= CHUNK) == (parity == 1)
    mask = jnp.logical_or(jnp.logical_and(is_cur, kl <= qi),
                          jnp.logical_and(jnp.logical_not(is_cur), jnp.logical_and(kl >= qi, n > 0)))
    lane = lax.broadcasted_iota(jnp.int32, (CHUNK, LANES), 1)
    lse_all = jnp.zeros((CHUNK, LANES), F32)
    for h0 in range(0, DIL_HEADS, DIL_HEAD_GROUP):
        heads = range(h0, h0 + DIL_HEAD_GROUP)
        sl = {h: slice(h * DIL_HEAD_DIM, (h + 1) * DIL_HEAD_DIM) for h in heads}
        sc = {h: jnp.where(mask, _dot_nt(q_ref[:, sl[h]], kbuf[:, sl[h]]), -jnp.inf) for h in heads}
        mx = {h: jnp.max(sc[h], axis=-1, keepdims=True) for h in heads}
        p = {h: jnp.exp(sc[h] - mx[h]).astype(BF16) for h in heads}
        res = {h: _dot(p[h], vbuf[:, h * ext:(h + 1) * ext]) for h in heads}
        for h in heads:
            den = res[h][:, DIL_HEAD_DIM:]
            o_ref[:, sl[h]] = (res[h][:, :DIL_HEAD_DIM] / den).astype(o_ref.dtype)
            lse_all = jnp.where(lane == h, mx[h] + jnp.log(den), lse_all)
    lse_ref[...] = lse_all


def _dil_group(qkv, tiles_qkv, dil, batch):
    tiles = qkv.shape[0]
    nb = tiles // batch

    def part(tile):
        return pl.BlockSpec((None, None, CHUNK, DIL_WIDTH), lambda bi, r, n: (bi * nb + n, r, 0, tile))

    return pl.pallas_call(
        _dil_kernel,
        grid=(batch, dil, nb),
        in_specs=[part(t) for t in tiles_qkv],
        out_specs=[pl.BlockSpec((None, None, CHUNK, DIL_WIDTH), lambda bi, r, n: (bi * nb + n, r, 0, 0)),
                   pl.BlockSpec((None, None, CHUNK, LANES), lambda bi, r, n: (bi * nb + n, r, 0, 0))],
        out_shape=[jax.ShapeDtypeStruct((tiles, dil, CHUNK, DIL_WIDTH), BF16),
                   jax.ShapeDtypeStruct((tiles, dil, CHUNK, LANES), F32)],
        scratch_shapes=[pltpu.VMEM((2 * CHUNK, DIL_WIDTH), BF16), pltpu.VMEM((2 * CHUNK, 2 * DIL_WIDTH), BF16)],
        compiler_params=_cparams(("parallel", "parallel", "arbitrary")),
        name=f"dilated_attention_d{dil}",
    )(qkv, qkv, qkv)


def _dil_combine_kernel(o0_ref, o1_ref, o2_ref, l0_ref, l1_ref, l2_ref, g_ref, y_ref, ob1, ob2, lb1, lb2):
    for (o_ref, l_ref, ob, lb, dil) in ((o1_ref, l1_ref, ob1, lb1, DIL_DILATIONS[1]),
                                        (o2_ref, l2_ref, ob2, lb2, DIL_DILATIONS[2])):
        rows = o_ref.shape[1]
        for r in range(dil):
            lb[pl.ds(r, rows, stride=dil), :] = l_ref[r]
            for h in range(DIL_HEADS):
                hs = slice(h * DIL_HEAD_DIM, (h + 1) * DIL_HEAD_DIM)
                ob[h, pl.ds(r, rows, stride=dil), :] = o_ref[r, :, hs].astype(F32)

    l0, l1, l2 = l0_ref[...], lb1[...], lb2[...]
    m = jnp.maximum(jnp.maximum(l0, l1), l2)
    e0, e1, e2 = jnp.exp(l0 - m), jnp.exp(l1 - m), jnp.exp(l2 - m)
    inv = 1.0 / (e0 + e1 + e2)
    w0, w1, w2 = e0 * inv, e1 * inv, e2 * inv
    rows = l0.shape[0]
    for h in range(DIL_HEADS):
        hs = slice(h * DIL_HEAD_DIM, (h + 1) * DIL_HEAD_DIM)

        def lane(w):
            return jnp.broadcast_to(w[:, h:h + 1], (rows, DIL_HEAD_DIM))

        o = lane(w0) * o0_ref[:, hs].astype(F32) + lane(w1) * ob1[h] + lane(w2) * ob2[h]
        y_ref[:, hs] = (_silu(g_ref[:, hs].astype(F32)) * o).astype(y_ref.dtype)


def _dil_combine(outs, lses, proj2d):
    m = proj2d.shape[0]
    d1, d2 = DIL_DILATIONS[1], DIL_DILATIONS[2]
    tm = CHUNK * d1
    sub = tm // d2
    per = (CHUNK * d2) // tm
    o_spec = pl.BlockSpec((tm, DIL_WIDTH), lambda i: (i, 0))
    l_spec = pl.BlockSpec((tm, LANES), lambda i: (i, 0))

    def grouped(width):
        return (pl.BlockSpec((None, d1, CHUNK, width), lambda i: (i, 0, 0, 0)),
                pl.BlockSpec((None, d2, sub, width), lambda i: (i // per, 0, i % per, 0)))

    o1_spec, o2_spec = grouped(DIL_WIDTH)
    l1_spec, l2_spec = grouped(LANES)
    return pl.pallas_call(
        _dil_combine_kernel,
        grid=(m // tm,),
        in_specs=[o_spec, o1_spec, o2_spec, l_spec, l1_spec, l2_spec,
                  pl.BlockSpec((tm, DIL_WIDTH), lambda i: (i, T_DG))],
        out_specs=o_spec,
        out_shape=jax.ShapeDtypeStruct((m, DIL_WIDTH), BF16),
        scratch_shapes=[pltpu.VMEM((DIL_HEADS, tm, DIL_HEAD_DIM), F32), pltpu.VMEM((DIL_HEADS, tm, DIL_HEAD_DIM), F32),
                        pltpu.VMEM((tm, LANES), F32), pltpu.VMEM((tm, LANES), F32)],
        compiler_params=_cparams(("parallel",)),
        name="dilated_combine",
    )(outs[0].reshape(m, DIL_WIDTH), outs[1], outs[2], lses[0].reshape(m, LANES), lses[1], lses[2], proj2d)


def _merge_kernel(ya_ref, yb_ref, yc_ref, wa_ref, wb_ref, wc_ref, ga_ref, gb_ref, gc_ref, o_ref):
    acc = _sigmoid(ga_ref[...].astype(F32)) * _dot(ya_ref[...], wa_ref[...])
    acc = acc + _sigmoid(gb_ref[...].astype(F32)) * _dot(yb_ref[...], wb_ref[...])
    acc = acc + _sigmoid(gc_ref[...].astype(F32)) * _dot(yc_ref[...], wc_ref[...])
    o_ref[...] = acc.astype(o_ref.dtype)


def _merge(ya, yb, yc, wa, wb, wc, proj2d, tm=512, tn=PROJ_TILE):
    m = ya.shape[0]
    d = wa.shape[1]

    def act(width):
        return pl.BlockSpec((tm, width), lambda j, i: (i, 0))

    def wgt(width):
        return pl.BlockSpec((width, tn), lambda j, i: (0, j))

    def gate(branch):
        return pl.BlockSpec((tm, tn), lambda j, i: (i, T_MG + branch * (d // tn) + j))

    return pl.pallas_call(
        _merge_kernel,
        grid=(d // tn, m // tm),
        in_specs=[act(SSD_INNER), act(RET_V), act(DIL_WIDTH), wgt(SSD_INNER), wgt(RET_V), wgt(DIL_WIDTH),
                  gate(0), gate(1), gate(2)],
        out_specs=pl.BlockSpec((tm, tn), lambda j, i: (i, j)),
        out_shape=jax.ShapeDtypeStruct((m, d), BF16),
        compiler_params=_cparams(("parallel", "parallel")),
        name="gated_merge",
    )(ya, yb, yc, wa, wb, wc, proj2d, proj2d, proj2d)


def _out_kernel(a_ref, w_ref, x_ref, g_ref, xo_ref, ho_ref):
    x = x_ref[...] + _dot(a_ref[...], w_ref[...])
    xo_ref[...] = x
    r = lax.rsqrt(jnp.mean(x * x, axis=-1, keepdims=True) + NORM_EPS)
    ho_ref[...] = (x * r * g_ref[...]).astype(ho_ref.dtype)


def _out_proj(merged, w_out, x2d, g_next, norm_dtype, tm=512):
    m, d = x2d.shape
    row = pl.BlockSpec((tm, d), lambda i: (i, 0))
    return pl.pallas_call(
        _out_kernel,
        grid=(m // tm,),
        in_specs=[row, pl.BlockSpec((d, d), lambda i: (0, 0)), row, pl.BlockSpec((1, d), lambda i: (0, 0))],
        out_specs=[row, row],
        out_shape=[jax.ShapeDtypeStruct((m, d), F32), jax.ShapeDtypeStruct((m, d), norm_dtype)],
        compiler_params=_cparams(("parallel",)),
        name="out_proj_residual_norm",
    )(merged, w_out, x2d, g_next.reshape(1, d))


def _split_w_in(w):
    z, xs, bc = w[:, 0:2048], w[:, 2048:4096], w[:, 4096:5120]
    dt = w[:, 5120:5152]
    rq, rk, rv, rg = w[:, 5152:6176], w[:, 6176:7200], w[:, 7200:9248], w[:, 9248:11296]
    dq0, dq1, dq2 = w[:, 11296:14368], w[:, 14368:17440], w[:, 17440:20512]
    dg, mg = w[:, 20512:21536], w[:, 21536:27680]
    q0, k0, v0 = dq0[:, :DIL_WIDTH], dq0[:, DIL_WIDTH:2 * DIL_WIDTH], dq0[:, 2 * DIL_WIDTH:]
    main = jnp.concatenate([z, xs, rv, rg, mg, q0, rk, k0, rq, v0, dg, bc], axis=1).astype(BF16)
    dt = jnp.pad(dt, ((0, 0), (0, LANES - SSD_HEADS))).astype(BF16)
    return main, dt, dq1.astype(BF16), dq2.astype(BF16)


def kernel(x, norm_g, w_in, conv_w, conv_b, dt_bias, a_log, d_skip, ssd_norm_g,
           w_o_ssd, w_o_ret, w_o_dil, w_out, final_norm_g):
    b, s, d = x.shape
    depth = w_in.shape[0]
    m = b * s
    assert s % (max(DIL_DILATIONS) * CHUNK) == 0 and d == 2048
    cosf, sinf = _rope_tables(s)

    x2d = x.reshape(m, d)
    h = _rms_norm(x2d, norm_g[0], BF16)
    for layer in range(depth):
        w_main, w_dt, w_dq1, w_dq2 = _split_w_in(w_in[layer])
        proj2d = _projection(h, w_main, cosf, sinf, rope_lo=ROPE_LO, rope_hi=ROPE_HI, n_scaled=ROPE_SCALED,
                             dil=1, name="in_proj")
        dt_raw = _matmul(h, w_dt, F32, PROJ_TILE, LANES, "dt_proj")
        proj = proj2d.reshape(b, s, N_MAIN)

        y_a = _ssd_branch(proj, dt_raw.reshape(b, s, LANES), conv_w[layer], conv_b[layer], dt_bias[layer],
                          a_log[layer], d_skip[layer], ssd_norm_g[layer])
        y_b = _ret_branch(proj)

        outs, lses = [], []
        for gi, dil in enumerate(DIL_DILATIONS):
            if dil == 1:
                qkv, tiles_qkv = proj2d.reshape(m // CHUNK, 1, CHUNK, N_MAIN), (T_DQ, T_DK, T_DV)
            else:
                w_g = w_dq1 if gi == 1 else w_dq2
                qkv = _projection(h, w_g, cosf, sinf, rope_lo=0, rope_hi=2, n_scaled=1, dil=dil,
                                  name=f"dqkv_proj_d{dil}")
                tiles_qkv = (0, 1, 2)
            o, lse = _dil_group(qkv, tiles_qkv, dil, b)
            outs.append(o)
            lses.append(lse)
        y_c = _dil_combine(outs, lses, proj2d)

        merged = _merge(y_a.reshape(m, SSD_INNER), y_b.reshape(m, RET_V), y_c,
                        w_o_ssd[layer].astype(BF16), w_o_ret[layer].astype(BF16), w_o_dil[layer].astype(BF16),
                        proj2d)
        last = layer == depth - 1
        g_next = final_norm_g if last else norm_g[layer + 1]
        x2d, h = _out_proj(merged, w_out[layer].astype(BF16), x2d, g_next, F32 if last else BF16)
    return h.reshape(b, s, d)
```

```python
import functools
import math

import numpy as np
import jax
import jax.numpy as jnp
from jax import lax
from jax.experimental import pallas as pl
from jax.experimental.pallas import tpu as pltpu

F32 = jnp.float32
BF16 = jnp.bfloat16

NORM_EPS = 1e-6
CHUNK = 128
ROPE_THETA = 10000.0
LOG2E = 1.4426950408889634

SSD_INNER = 2048
SSD_HEAD_DIM = 64
SSD_HEADS = 32
SSD_GROUPS = 4
SSD_REP = SSD_HEADS // SSD_GROUPS
SSD_STATE = 128
SSD_CONV = 4
SSD_BC = SSD_GROUPS * SSD_STATE

RET_HEADS = 8
RET_QK_DIM = 128
RET_V_DIM = 256
RET_QK = RET_HEADS * RET_QK_DIM
RET_V = RET_HEADS * RET_V_DIM

DIL_DILATIONS = (1, 4, 16)
DIL_HEADS = 8
DIL_HEAD_DIM = 128
DIL_WIDTH = DIL_HEADS * DIL_HEAD_DIM
DIL_HEAD_GROUP = 4
QK_SCALE = DIL_HEAD_DIM ** -0.5

LANES = 128
MXU_COLS = 256
VMEM_LIMIT = 56 * 1024 * 1024
PROJ_TILE = 1024

W_DT_COL = 5120
W_AFTER_DT = W_DT_COL + SSD_HEADS
W_DQ1_COL = W_AFTER_DT + 9 * PROJ_TILE
W_DQ2_COL = W_AFTER_DT + 12 * PROJ_TILE

T_Z, T_XS, T_BC, T_RQ, T_RK, T_RV, T_RG, T_DQ, T_DK, T_DV, T_DG, T_MG = 0, 2, 4, 5, 6, 7, 9, 11, 12, 13, 14, 15
N_MAIN_TILES = 21
N_MAIN = N_MAIN_TILES * PROJ_TILE


def _main_src_col(j):
    sub = 8
    return (j * (PROJ_TILE // sub) + jnp.where(j < T_RQ, 0, SSD_HEADS // sub)
            + jnp.where(j < T_DG, 0, 6 * PROJ_TILE // sub)) * sub


def _cparams(sem):
    return pltpu.CompilerParams(dimension_semantics=sem, vmem_limit_bytes=VMEM_LIMIT)


def _sigmoid(v):
    return 1.0 / (1.0 + jnp.exp(-v))


def _silu(v):
    return v * _sigmoid(v)


def _dot(a, b):
    return jnp.dot(a, b, preferred_element_type=F32)


def _dot_nt(a, b):
    return lax.dot_general(a, b, (((1,), (1,)), ((), ())), preferred_element_type=F32)


def _norm_kernel(x_ref, g_ref, o_ref):
    x = x_ref[...]
    r = lax.rsqrt(jnp.mean(x * x, axis=-1, keepdims=True) + NORM_EPS)
    o_ref[...] = (x * r * g_ref[...]).astype(o_ref.dtype)


def _rms_norm(x2d, g, out_dtype, tm=512):
    m, d = x2d.shape
    return pl.pallas_call(
        _norm_kernel,
        grid=(m // tm,),
        in_specs=[pl.BlockSpec((tm, d), lambda i: (i, 0)),
                  pl.BlockSpec((1, d), lambda i: (0, 0))],
        out_specs=pl.BlockSpec((tm, d), lambda i: (i, 0)),
        out_shape=jax.ShapeDtypeStruct((m, d), out_dtype),
        compiler_params=_cparams(("parallel",)),
        name="rms_norm",
    )(x2d, g.reshape(1, d))


def _dt_kernel(a_ref, wt_ref, o_ref):
    res = _dot_nt(a_ref[...], wt_ref[...].astype(BF16))
    lane = lax.broadcasted_iota(jnp.int32, res.shape, 1)
    o_ref[...] = jnp.where(lane < SSD_HEADS, res, 0.0)


def _dt_projection(a, w_t, layer, tm=PROJ_TILE):
    m, k = a.shape
    return pl.pallas_call(
        _dt_kernel,
        grid=(m // tm,),
        in_specs=[pl.BlockSpec((tm, k), lambda i: (i, 0)),
                  pl.BlockSpec((None, LANES, k), lambda i: (layer, W_DT_COL // LANES, 0))],
        out_specs=pl.BlockSpec((tm, LANES), lambda i: (i, 0)),
        out_shape=jax.ShapeDtypeStruct((m, LANES), F32),
        compiler_params=_cparams(("parallel",)),
        name="dt_proj",
    )(a, w_t)


def _proj_kernel(a_ref, wt_ref, cos_ref, sin_ref, o_ref, wbuf, *scratch, rope_tiles, scaled_tiles, dil):
    j = pl.program_id(0)

    @pl.when(pl.program_id(1) == 0)
    def _():
        for c0 in range(0, wbuf.shape[1], LANES):
            wbuf[:, c0:c0 + LANES] = wt_ref[c0:c0 + LANES, :].T.astype(BF16)

    def any_of(tiles):
        hit = j == tiles[0]
        for t in tiles[1:]:
            hit = jnp.logical_or(hit, j == t)
        return hit

    def body(rope):
        if rope:
            factor = jnp.where(any_of(scaled_tiles), QK_SCALE, 1.0)
            cosf = cos_ref[...] * factor
            sinf = sin_ref[...] * factor
        tn = wbuf.shape[1]
        width = tn if dil == 1 else MXU_COLS
        for c0 in range(0, tn, width):
            res = _dot(a_ref[...], wbuf[:, c0:c0 + width])
            for c in range(width // LANES):
                cs = slice(c0 + c * LANES, c0 + (c + 1) * LANES)
                blk = res[:, c * LANES:(c + 1) * LANES]
                if rope:
                    blk = blk * cosf + pltpu.roll(blk, DIL_HEAD_DIM // 2, 1) * sinf
                if dil == 1:
                    o_ref[:, cs] = blk.astype(o_ref.dtype)
                else:
                    acc = scratch[0]
                    slot = (c0 // width % 2) * (width // LANES) + c
                    acc[slot] = blk
                    for r in range(dil):
                        o_ref[r, :, cs] = acc[slot, pl.ds(r, CHUNK, stride=dil), :].astype(o_ref.dtype)

    is_rope = any_of(rope_tiles)
    pl.when(is_rope)(functools.partial(body, True))
    pl.when(jnp.logical_not(is_rope))(functools.partial(body, False))


def _projection(a, w_t, layer, cosf, sinf, *, n_tiles, src_col, rope_tiles, scaled_tiles, dil, name):
    m, k = a.shape
    s = cosf.shape[0]
    tn = PROJ_TILE
    n = n_tiles * tn
    tm = PROJ_TILE if dil == 1 else CHUNK * dil
    tab = pl.BlockSpec((tm, DIL_HEAD_DIM), lambda j, i: (i % (s // tm), 0))
    if dil == 1:
        out_spec = pl.BlockSpec((tm, tn), lambda j, i: (i, j))
        out_shape = jax.ShapeDtypeStruct((m, n), BF16)
        scratch = []
    else:
        out_spec = pl.BlockSpec((None, dil, CHUNK, tn), lambda j, i: (i, 0, 0, j))
        out_shape = jax.ShapeDtypeStruct((m // tm, dil, CHUNK, n), BF16)
        scratch = [pltpu.VMEM((2 * MXU_COLS // LANES, tm, LANES), F32)]
    return pl.pallas_call(
        functools.partial(_proj_kernel, rope_tiles=rope_tiles, scaled_tiles=scaled_tiles, dil=dil),
        grid=(n_tiles, m // tm),
        in_specs=[pl.BlockSpec((tm, k), lambda j, i: (i, 0)),
                  pl.BlockSpec((None, pl.Element(tn), pl.Element(k)), lambda j, i: (layer, src_col(j), 0),
                               pipeline_mode=pl.Buffered(1)),
                  tab, tab],
        out_specs=out_spec,
        out_shape=out_shape,
        scratch_shapes=[pltpu.VMEM((k, tn), BF16)] + scratch,
        compiler_params=_cparams(("parallel", "arbitrary")),
        name=name,
    )(a, w_t, cosf, sinf)


def _ssd_kernel(z_ref, xs_ref, bc_ref, dt_ref, cwx_ref, cwbc_ref, cbx_ref, cbbc_ref,
                dtb_ref, alog_ref, dskip_ref, ng_ref, o_ref, xbuf, bcbuf, state, ybuf):
    c = pl.program_id(1)

    @pl.when(c == 0)
    def _():
        xbuf[CHUNK:, :] = jnp.zeros((CHUNK, SSD_INNER), BF16)
        bcbuf[CHUNK:, :] = jnp.zeros((CHUNK, 2 * SSD_BC), BF16)
        state[...] = jnp.zeros_like(state)

    srow = lax.broadcasted_iota(jnp.int32, (CHUNK, 2 * CHUNK), 0)
    scol = lax.broadcasted_iota(jnp.int32, (CHUNK, 2 * CHUNK), 1)
    shifts = [jnp.where(scol == ((srow - back) & (2 * CHUNK - 1)), 1.0, 0.0).astype(BF16)
              for back in range(1, SSD_CONV)]

    def conv_silu(in_ref, buf, w_ref, b_ref):
        cur = in_ref[...]
        buf[0:CHUNK, :] = cur
        ext = buf[...]
        acc = b_ref[...] + w_ref[SSD_CONV - 1:SSD_CONV, :] * cur.astype(F32)
        for back in range(1, SSD_CONV):
            tap = SSD_CONV - 1 - back
            acc = acc + w_ref[tap:tap + 1, :] * _dot(shifts[back - 1], ext)
        buf[CHUNK:, :] = cur
        return _silu(acc)

    xs = conv_silu(xs_ref, xbuf, cwx_ref, cbx_ref)
    bc = conv_silu(bc_ref, bcbuf, cwbc_ref, cbbc_ref)
    xs_b = xs.astype(BF16)

    v = dt_ref[...] + dtb_ref[...]
    dt = jnp.maximum(v, 0.0) + jnp.log1p(jnp.exp(-jnp.abs(v)))
    a_neg = -jnp.exp(alog_ref[...])
    da = dt * a_neg
    row = lax.broadcasted_iota(jnp.int32, (CHUNK, CHUNK), 0)
    col = lax.broadcasted_iota(jnp.int32, (CHUNK, CHUNK), 1)
    causal = row >= col
    tril = jnp.where(causal, 1.0, 0.0).astype(F32)
    acs = jnp.dot(tril, da, preferred_element_type=F32, precision=lax.Precision.HIGHEST)
    acs2 = acs * LOG2E
    ldt = jnp.log2(dt)
    arow_t = (acs2 - ldt).T
    total2 = acs2[CHUNK - 1:CHUNK, :]
    w_t = jnp.exp2(total2 - acs2 + ldt).T
    lane_lo = lax.broadcasted_iota(jnp.int32, (CHUNK, LANES), 1) < SSD_HEAD_DIM

    for g in range(SSD_GROUPS):
        bm_g = bc[:, g * SSD_STATE:(g + 1) * SSD_STATE]
        cm_g = bc[:, SSD_BC + g * SSD_STATE:SSD_BC + (g + 1) * SSD_STATE]
        cb = _dot_nt(cm_g.astype(BF16), bm_g.astype(BF16))
        bm_gt = bm_g.T
        for pair in range(SSD_REP // 2):
            h0 = g * SSD_REP + 2 * pair
            ps = slice(h0 * SSD_HEAD_DIM, (h0 + 2) * SSD_HEAD_DIM)
            xs_p = xs_b[:, ps]
            st_old = state[h0 // 2]
            rhs = jnp.concatenate([xs_p, st_old.astype(BF16)], axis=0)
            ys, sts, cds = [], [], []
            for h in (h0, h0 + 1):
                a_col = jnp.broadcast_to(acs2[:, h:h + 1], (CHUNK, CHUNK))
                decay = jnp.exp2(jnp.where(causal, a_col - arow_t[h:h + 1, :], -jnp.inf))
                e_col = jnp.exp2(a_col)
                lhs = jnp.concatenate([(cb * decay).astype(BF16), (cm_g * e_col).astype(BF16)], axis=1)
                ys.append(_dot(lhs, rhs))
                sts.append(_dot((bm_gt * w_t[h:h + 1, :]).astype(BF16), xs_p))
                cds.append(e_col[CHUNK - 1:CHUNK, :])
            ybuf[:, ps] = jnp.where(lane_lo, ys[0], ys[1])
            chunk_decay = jnp.where(lane_lo[0:1, :], cds[0], cds[1])
            state[h0 // 2] = st_old * chunk_decay + jnp.where(lane_lo, sts[0], sts[1])

    y = (ybuf[...] + xs * dskip_ref[...]) * _silu(z_ref[...].astype(F32))
    r = lax.rsqrt(jnp.mean(y * y, axis=-1, keepdims=True) + NORM_EPS)
    o_ref[...] = (y * r * ng_ref[...]).astype(o_ref.dtype)


def _ssd_branch(proj, dt_raw, conv_w, conv_b, dt_bias, a_log, d_skip, norm_g):
    b, s, _ = proj.shape
    nc = s // CHUNK

    def col(width, tile):
        return pl.BlockSpec((None, CHUNK, width), lambda bi, ci: (bi, ci, tile * PROJ_TILE // width))

    def const(shape):
        return pl.BlockSpec(shape, lambda bi, ci: (0,) * len(shape))

    pad = LANES - SSD_HEADS
    dtb = jnp.pad(dt_bias, (0, pad))[None, :]
    alog = jnp.pad(a_log, (0, pad))[None, :]
    dskip = jnp.repeat(d_skip, SSD_HEAD_DIM)[None, :]
    return pl.pallas_call(
        _ssd_kernel,
        grid=(b, nc),
        in_specs=[col(SSD_INNER, T_Z), col(SSD_INNER, T_XS), col(2 * SSD_BC, T_BC),
                  pl.BlockSpec((None, CHUNK, LANES), lambda bi, ci: (bi, ci, 0)),
                  const((SSD_CONV, SSD_INNER)), const((SSD_CONV, 2 * SSD_BC)),
                  const((1, SSD_INNER)), const((1, 2 * SSD_BC)),
                  const((1, LANES)), const((1, LANES)), const((1, SSD_INNER)), const((1, SSD_INNER))],
        out_specs=pl.BlockSpec((None, CHUNK, SSD_INNER), lambda bi, ci: (bi, ci, 0)),
        out_shape=jax.ShapeDtypeStruct((b, s, SSD_INNER), BF16),
        scratch_shapes=[pltpu.VMEM((2 * CHUNK, SSD_INNER), BF16),
                        pltpu.VMEM((2 * CHUNK, 2 * SSD_BC), BF16),
                        pltpu.VMEM((SSD_HEADS // 2, SSD_STATE, 2 * SSD_HEAD_DIM), F32),
                        pltpu.VMEM((CHUNK, SSD_INNER), F32)],
        compiler_params=_cparams(("parallel", "arbitrary")),
        name="ssd_mixer",
    )(proj, proj, proj, dt_raw, conv_w[:, :SSD_INNER], conv_w[:, SSD_INNER:],
      conv_b[None, :SSD_INNER], conv_b[None, SSD_INNER:], dtb, alog, dskip, norm_g[None, :])


def _rope_tables(s):
    inv_freq = ROPE_THETA ** (-jnp.arange(0, RET_QK_DIM, 2, dtype=F32) / RET_QK_DIM)
    ang = jnp.arange(s, dtype=F32)[:, None] * inv_freq[None, :]
    cos, sin = jnp.cos(ang), jnp.sin(ang)
    return jnp.concatenate([cos, cos], axis=-1), jnp.concatenate([-sin, sin], axis=-1)


_LOG_GAMMA = [float(np.log(np.float32(1.0) - np.exp2(np.float32(-5.0 - h)))) for h in range(RET_HEADS)]


def _ret_kernel(q_ref, k_ref, v_lo_ref, v_hi_ref, g_lo_ref, g_hi_ref, o_ref, state, dmask, qdec, kdec):
    c = pl.program_id(1)

    @pl.when(c == 0)
    def _():
        state[...] = jnp.zeros_like(state)
        li = lax.broadcasted_iota(jnp.int32, (CHUNK, CHUNK), 0).astype(F32)
        si = lax.broadcasted_iota(jnp.int32, (CHUNK, CHUNK), 1).astype(F32)
        rel = li - si
        for h in range(RET_HEADS):
            lg = _LOG_GAMMA[h]
            dmask[h] = jnp.where(rel >= 0, jnp.exp(jnp.maximum(rel, 0.0) * lg), 0.0)
            qdec[h] = jnp.exp((li + 1.0) * lg)
            kdec[h] = jnp.exp((CHUNK - 1.0 - li) * lg)

    for h in range(RET_HEADS):
        qs = slice(h * RET_QK_DIM, (h + 1) * RET_QK_DIM)
        vs = slice(h * RET_V_DIM, (h + 1) * RET_V_DIM)
        half = RET_HEADS // 2
        v_ref, g_ref = (v_lo_ref, g_lo_ref) if h < half else (v_hi_ref, g_hi_ref)
        hv = slice((h % half) * RET_V_DIM, (h % half + 1) * RET_V_DIM)
        qr = q_ref[:, qs]
        kr = k_ref[:, qs]
        v_h = v_ref[:, hv]
        scores = _dot_nt(qr, kr) * dmask[h]
        inner = _dot(scores.astype(BF16), v_h)
        st_old = state[h]
        cross = _dot((qr.astype(F32) * qdec[h]).astype(BF16), st_old.astype(BF16))
        kv = _dot((kr.astype(F32) * kdec[h]).T.astype(BF16), v_h)
        state[h] = st_old * math.exp(CHUNK * _LOG_GAMMA[h]) + kv
        o = inner + cross
        o = o * lax.rsqrt(jnp.mean(o * o, axis=-1, keepdims=True) + NORM_EPS)
        o_ref[:, vs] = (_silu(g_ref[:, hv].astype(F32)) * o).astype(o_ref.dtype)


def _ret_branch(proj):
    b, s, _ = proj.shape
    nc = s // CHUNK

    def col(tile):
        return pl.BlockSpec((None, CHUNK, PROJ_TILE), lambda bi, ci: (bi, ci, tile))

    return pl.pallas_call(
        _ret_kernel,
        grid=(b, nc),
        in_specs=[col(T_RQ), col(T_RK), col(T_RV), col(T_RV + 1), col(T_RG), col(T_RG + 1)],
        out_specs=pl.BlockSpec((None, CHUNK, RET_V), lambda bi, ci: (bi, ci, 0)),
        out_shape=jax.ShapeDtypeStruct((b, s, RET_V), BF16),
        scratch_shapes=[pltpu.VMEM((RET_HEADS, RET_QK_DIM, RET_V_DIM), F32),
                        pltpu.VMEM((RET_HEADS, CHUNK, CHUNK), F32),
                        pltpu.VMEM((RET_HEADS, CHUNK, CHUNK), F32),
                        pltpu.VMEM((RET_HEADS, CHUNK, CHUNK), F32)],
        compiler_params=_cparams(("parallel", "arbitrary")),
        name="retention_mixer",
    )(proj, proj, proj, proj, proj, proj)


def _dil_kernel(q_ref, k_ref, v_ref, o_ref, lse_ref, kbuf, vbuf):
    n = pl.program_id(2)
    ext = 2 * DIL_HEAD_DIM

    @pl.when(n == 0)
    def _():
        kbuf[...] = jnp.zeros_like(kbuf)
        for h in range(DIL_HEADS):
            vbuf[:, h * ext:h * ext + DIL_HEAD_DIM] = jnp.zeros((2 * CHUNK, DIL_HEAD_DIM), BF16)
            vbuf[:, h * ext + DIL_HEAD_DIM:(h + 1) * ext] = jnp.ones((2 * CHUNK, DIL_HEAD_DIM), BF16)

    parity = n % 2
    off = pl.multiple_of(parity * CHUNK, CHUNK)
    kbuf[pl.ds(off, CHUNK), :] = k_ref[...]
    for h in range(DIL_HEADS):
        vbuf[pl.ds(off, CHUNK), h * ext:h * ext + DIL_HEAD_DIM] = v_ref[:, h * DIL_HEAD_DIM:(h + 1) * DIL_HEAD_DIM]

    qi = lax.broadcasted_iota(jnp.int32, (CHUNK, 2 * CHUNK), 0)
    kj = lax.broadcasted_iota(jnp.int32, (CHUNK, 2 * CHUNK), 1)
    kl = kj & (CHUNK - 1)
    is_cur = (kj >= CHUNK) == (parity == 1)
    mask = jnp.logical_or(jnp.logical_and(is_cur, kl <= qi),
                          jnp.logical_and(jnp.logical_not(is_cur), jnp.logical_and(kl >= qi, n > 0)))
    lane = lax.broadcasted_iota(jnp.int32, (CHUNK, LANES), 1)
    lse_all = jnp.zeros((CHUNK, LANES), F32)
    for h0 in range(0, DIL_HEADS, DIL_HEAD_GROUP):
        heads = range(h0, h0 + DIL_HEAD_GROUP)
        sl = {h: slice(h * DIL_HEAD_DIM, (h + 1) * DIL_HEAD_DIM) for h in heads}
        sc = {h: jnp.where(mask, _dot_nt(q_ref[:, sl[h]], kbuf[:, sl[h]]), -jnp.inf) for h in heads}
        mx = {h: jnp.max(sc[h], axis=-1, keepdims=True) for h in heads}
        p = {h: jnp.exp(sc[h] - mx[h]).astype(BF16) for h in heads}
        res = {h: _dot(p[h], vbuf[:, h * ext:(h + 1) * ext]) for h in heads}
        for h in heads:
            den = res[h][:, DIL_HEAD_DIM:]
            o_ref[:, sl[h]] = (res[h][:, :DIL_HEAD_DIM] / den).astype(o_ref.dtype)
            lse_all = jnp.where(lane == h, mx[h] + jnp.log(den), lse_all)
    lse_ref[...] = lse_all


def _dil_group(qkv, tiles_qkv, dil, batch):
    tiles = qkv.shape[0]
    nb = tiles // batch

    def part(tile):
        return pl.BlockSpec((None, None, CHUNK, DIL_WIDTH), lambda bi, r, n: (bi * nb + n, r, 0, tile))

    return pl.pallas_call(
        _dil_kernel,
        grid=(batch, dil, nb),
        in_specs=[part(t) for t in tiles_qkv],
        out_specs=[pl.BlockSpec((None, None, CHUNK, DIL_WIDTH), lambda bi, r, n: (bi * nb + n, r, 0, 0)),
                   pl.BlockSpec((None, None, CHUNK, LANES), lambda bi, r, n: (bi * nb + n, r, 0, 0))],
        out_shape=[jax.ShapeDtypeStruct((tiles, dil, CHUNK, DIL_WIDTH), BF16),
                   jax.ShapeDtypeStruct((tiles, dil, CHUNK, LANES), F32)],
        scratch_shapes=[pltpu.VMEM((2 * CHUNK, DIL_WIDTH), BF16), pltpu.VMEM((2 * CHUNK, 2 * DIL_WIDTH), BF16)],
        compiler_params=_cparams(("parallel", "parallel", "arbitrary")),
        name=f"dilated_attention_d{dil}",
    )(qkv, qkv, qkv)


def _dil_combine_kernel(o0_ref, o1_ref, o2_ref, l0_ref, l1_ref, l2_ref, g_ref, y_ref, ob1, ob2, lb1, lb2):
    for (o_ref, l_ref, ob, lb, dil) in ((o1_ref, l1_ref, ob1, lb1, DIL_DILATIONS[1]),
                                        (o2_ref, l2_ref, ob2, lb2, DIL_DILATIONS[2])):
        rows = o_ref.shape[1]
        for r in range(dil):
            lb[pl.ds(r, rows, stride=dil), :] = l_ref[r]
            for h in range(DIL_HEADS):
                hs = slice(h * DIL_HEAD_DIM, (h + 1) * DIL_HEAD_DIM)
                ob[h, pl.ds(r, rows, stride=dil), :] = o_ref[r, :, hs].astype(F32)

    l0, l1, l2 = l0_ref[...], lb1[...], lb2[...]
    m = jnp.maximum(jnp.maximum(l0, l1), l2)
    e0, e1, e2 = jnp.exp(l0 - m), jnp.exp(l1 - m), jnp.exp(l2 - m)
    inv = 1.0 / (e0 + e1 + e2)
    w0, w1, w2 = e0 * inv, e1 * inv, e2 * inv
    rows = l0.shape[0]
    for h in range(DIL_HEADS):
        hs = slice(h * DIL_HEAD_DIM, (h + 1) * DIL_HEAD_DIM)

        def lane(w):
            return jnp.broadcast_to(w[:, h:h + 1], (rows, DIL_HEAD_DIM))

        o = lane(w0) * o0_ref[:, hs].astype(F32) + lane(w1) * ob1[h] + lane(w2) * ob2[h]
        y_ref[:, hs] = (_silu(g_ref[:, hs].astype(F32)) * o).astype(y_ref.dtype)


def _dil_combine(outs, lses, proj2d):
    m = proj2d.shape[0]
    d1, d2 = DIL_DILATIONS[1], DIL_DILATIONS[2]
    tm = CHUNK * d1
    sub = tm // d2
    per = (CHUNK * d2) // tm
    o_spec = pl.BlockSpec((tm, DIL_WIDTH), lambda i: (i, 0))
    l_spec = pl.BlockSpec((tm, LANES), lambda i: (i, 0))

    def grouped(width):
        return (pl.BlockSpec((None, d1, CHUNK, width), lambda i: (i, 0, 0, 0)),
                pl.BlockSpec((None, d2, sub, width), lambda i: (i // per, 0, i % per, 0)))

    o1_spec, o2_spec = grouped(DIL_WIDTH)
    l1_spec, l2_spec = grouped(LANES)
    return pl.pallas_call(
        _dil_combine_kernel,
        grid=(m // tm,),
        in_specs=[o_spec, o1_spec, o2_spec, l_spec, l1_spec, l2_spec,
                  pl.BlockSpec((tm, DIL_WIDTH), lambda i: (i, T_DG))],
        out_specs=o_spec,
        out_shape=jax.ShapeDtypeStruct((m, DIL_WIDTH), BF16),
        scratch_shapes=[pltpu.VMEM((DIL_HEADS, tm, DIL_HEAD_DIM), F32), pltpu.VMEM((DIL_HEADS, tm, DIL_HEAD_DIM), F32),
                        pltpu.VMEM((tm, LANES), F32), pltpu.VMEM((tm, LANES), F32)],
        compiler_params=_cparams(("parallel",)),
        name="dilated_combine",
    )(outs[0].reshape(m, DIL_WIDTH), outs[1], outs[2], lses[0].reshape(m, LANES), lses[1], lses[2], proj2d)


def _merge_kernel(ya_ref, yb_ref, yc_ref, wa_ref, wb_ref, wc_ref, ga_ref, gb_ref, gc_ref, o_ref):
    acc = _sigmoid(ga_ref[...].astype(F32)) * _dot(ya_ref[...], wa_ref[...])
    acc = acc + _sigmoid(gb_ref[...].astype(F32)) * _dot(yb_ref[...], wb_ref[...])
    acc = acc + _sigmoid(gc_ref[...].astype(F32)) * _dot(yc_ref[...], wc_ref[...])
    o_ref[...] = acc.astype(o_ref.dtype)


def _merge(ya, yb, yc, wa, wb, wc, proj2d, tm=512, tn=PROJ_TILE):
    m = ya.shape[0]
    d = wa.shape[1]

    def act(width):
        return pl.BlockSpec((tm, width), lambda j, i: (i, 0))

    def wgt(width):
        return pl.BlockSpec((width, tn), lambda j, i: (0, j))

    def gate(branch):
        return pl.BlockSpec((tm, tn), lambda j, i: (i, T_MG + branch * (d // tn) + j))

    return pl.pallas_call(
        _merge_kernel,
        grid=(d // tn, m // tm),
        in_specs=[act(SSD_INNER), act(RET_V), act(DIL_WIDTH), wgt(SSD_INNER), wgt(RET_V), wgt(DIL_WIDTH),
                  gate(0), gate(1), gate(2)],
        out_specs=pl.BlockSpec((tm, tn), lambda j, i: (i, j)),
        out_shape=jax.ShapeDtypeStruct((m, d), BF16),
        compiler_params=_cparams(("parallel", "parallel")),
        name="gated_merge",
    )(ya, yb, yc, wa, wb, wc, proj2d, proj2d, proj2d)


def _out_kernel(a_ref, w_ref, x_ref, g_ref, xo_ref, ho_ref):
    x = x_ref[...] + _dot(a_ref[...], w_ref[...])
    xo_ref[...] = x
    r = lax.rsqrt(jnp.mean(x * x, axis=-1, keepdims=True) + NORM_EPS)
    ho_ref[...] = (x * r * g_ref[...]).astype(ho_ref.dtype)


def _out_proj(merged, w_out, x2d, g_next, norm_dtype, tm=512):
    m, d = x2d.shape
    row = pl.BlockSpec((tm, d), lambda i: (i, 0))
    return pl.pallas_call(
        _out_kernel,
        grid=(m // tm,),
        in_specs=[row, pl.BlockSpec((d, d), lambda i: (0, 0)), row, pl.BlockSpec((1, d), lambda i: (0, 0))],
        out_specs=[row, row],
        out_shape=[jax.ShapeDtypeStruct((m, d), F32), jax.ShapeDtypeStruct((m, d), norm_dtype)],
        compiler_params=_cparams(("parallel",)),
        name="out_proj_residual_norm",
    )(merged, w_out, x2d, g_next.reshape(1, d))


def kernel(x, norm_g, w_in, conv_w, conv_b, dt_bias, a_log, d_skip, ssd_norm_g,
           w_o_ssd, w_o_ret, w_o_dil, w_out, final_norm_g):
    b, s, d = x.shape
    depth = w_in.shape[0]
    m = b * s
    assert s % (max(DIL_DILATIONS) * CHUNK) == 0 and d == 2048
    cosf, sinf = _rope_tables(s)
    w_t = jnp.swapaxes(w_in, 1, 2)

    x2d = x.reshape(m, d)
    h = _rms_norm(x2d, norm_g[0], BF16)
    for layer in range(depth):
        proj2d = _projection(h, w_t, layer, cosf, sinf, n_tiles=N_MAIN_TILES, src_col=_main_src_col,
                             rope_tiles=(T_RQ, T_RK, T_DQ, T_DK), scaled_tiles=(T_RK, T_DQ), dil=1,
                             name="in_proj")
        dt_raw = _dt_projection(h, w_t, layer)
        proj = proj2d.reshape(b, s, N_MAIN)

        y_a = _ssd_branch(proj, dt_raw.reshape(b, s, LANES), conv_w[layer], conv_b[layer], dt_bias[layer],
                          a_log[layer], d_skip[layer], ssd_norm_g[layer])
        y_b = _ret_branch(proj)

        outs, lses = [], []
        for gi, dil in enumerate(DIL_DILATIONS):
            if dil == 1:
                qkv, tiles_qkv = proj2d.reshape(m // CHUNK, 1, CHUNK, N_MAIN), (T_DQ, T_DK, T_DV)
            else:
                first = W_DQ1_COL if gi == 1 else W_DQ2_COL
                qkv = _projection(h, w_t, layer, cosf, sinf, n_tiles=3,
                                  src_col=lambda j, f=first: (f // 8 + j * (PROJ_TILE // 8)) * 8,
                                  rope_tiles=(0, 1), scaled_tiles=(0,), dil=dil, name=f"dqkv_proj_d{dil}")
                tiles_qkv = (0, 1, 2)
            o, lse = _dil_group(qkv, tiles_qkv, dil, b)
            outs.append(o)
            lses.append(lse)
        y_c = _dil_combine(outs, lses, proj2d)

        merged = _merge(y_a.reshape(m, SSD_INNER), y_b.reshape(m, RET_V), y_c,
                        w_o_ssd[layer].astype(BF16), w_o_ret[layer].astype(BF16), w_o_dil[layer].astype(BF16),
                        proj2d)
        last = layer == depth - 1
        g_next = final_norm_g if last else norm_g[layer + 1]
        x2d, h = _out_proj(merged, w_out[layer].astype(BF16), x2d, g_next, F32 if last else BF16)
    return h.reshape(b, s, d)
```

```python
import functools
import math

import numpy as np
import jax
import jax.numpy as jnp
from jax import lax
from jax.experimental import pallas as pl
from jax.experimental.pallas import tpu as pltpu

F32 = jnp.float32
BF16 = jnp.bfloat16

NORM_EPS = 1e-6
CHUNK = 128
ROPE_THETA = 10000.0
LOG2E = 1.4426950408889634

SSD_INNER = 2048
SSD_HEAD_DIM = 64
SSD_HEADS = 32
SSD_GROUPS = 4
SSD_REP = SSD_HEADS // SSD_GROUPS
SSD_STATE = 128
SSD_CONV = 4
SSD_BC = SSD_GROUPS * SSD_STATE

RET_HEADS = 8
RET_QK_DIM = 128
RET_V_DIM = 256
RET_QK = RET_HEADS * RET_QK_DIM
RET_V = RET_HEADS * RET_V_DIM

DIL_DILATIONS = (1, 4, 16)
DIL_HEADS = 8
DIL_HEAD_DIM = 128
DIL_WIDTH = DIL_HEADS * DIL_HEAD_DIM
DIL_HEAD_GROUP = 4
QK_SCALE = DIL_HEAD_DIM ** -0.5

LANES = 128
MXU_COLS = 256
VMEM_LIMIT = 56 * 1024 * 1024
PROJ_TILE = 1024

W_DT_COL = 5120
W_AFTER_DT = W_DT_COL + SSD_HEADS
W_DQ1_COL = W_AFTER_DT + 9 * PROJ_TILE
W_DQ2_COL = W_AFTER_DT + 12 * PROJ_TILE

T_Z, T_XS, T_BC, T_RQ, T_RK, T_RV, T_RG, T_DQ, T_DK, T_DV, T_DG, T_MG = 0, 2, 4, 5, 6, 7, 9, 11, 12, 13, 14, 15
N_MAIN_TILES = 21
N_MAIN = N_MAIN_TILES * PROJ_TILE


def _main_src_col(j):
    sub = 8
    return (j * (PROJ_TILE // sub) + jnp.where(j < T_RQ, 0, SSD_HEADS // sub)
            + jnp.where(j < T_DG, 0, 6 * PROJ_TILE // sub)) * sub


def _cparams(sem):
    return pltpu.CompilerParams(dimension_semantics=sem, vmem_limit_bytes=VMEM_LIMIT)


def _sigmoid(v):
    return 1.0 / (1.0 + jnp.exp(-v))


def _silu(v):
    return v * _sigmoid(v)


def _dot(a, b):
    return jnp.dot(a, b, preferred_element_type=F32)


def _dot_nt(a, b):
    return lax.dot_general(a, b, (((1,), (1,)), ((), ())), preferred_element_type=F32)


def _norm_kernel(x_ref, g_ref, o_ref):
    x = x_ref[...]
    r = lax.rsqrt(jnp.mean(x * x, axis=-1, keepdims=True) + NORM_EPS)
    o_ref[...] = (x * r * g_ref[...]).astype(o_ref.dtype)


def _rms_norm(x2d, g, out_dtype, tm=512):
    m, d = x2d.shape
    return pl.pallas_call(
        _norm_kernel,
        grid=(m // tm,),
        in_specs=[pl.BlockSpec((tm, d), lambda i: (i, 0)),
                  pl.BlockSpec((1, d), lambda i: (0, 0))],
        out_specs=pl.BlockSpec((tm, d), lambda i: (i, 0)),
        out_shape=jax.ShapeDtypeStruct((m, d), out_dtype),
        compiler_params=_cparams(("parallel",)),
        name="rms_norm",
    )(x2d, g.reshape(1, d))


def _dt_kernel(a_ref, wt_ref, o_ref):
    res = _dot_nt(a_ref[...], wt_ref[...].astype(BF16))
    lane = lax.broadcasted_iota(jnp.int32, res.shape, 1)
    o_ref[...] = jnp.where(lane < SSD_HEADS, res, 0.0)


def _dt_projection(a, w_t, layer, tm=PROJ_TILE):
    m, k = a.shape
    return pl.pallas_call(
        _dt_kernel,
        grid=(m // tm,),
        in_specs=[pl.BlockSpec((tm, k), lambda i: (i, 0)),
                  pl.BlockSpec((None, LANES, k), lambda i: (layer, W_DT_COL // LANES, 0))],
        out_specs=pl.BlockSpec((tm, LANES), lambda i: (i, 0)),
        out_shape=jax.ShapeDtypeStruct((m, LANES), F32),
        compiler_params=_cparams(("parallel",)),
        name="dt_proj",
    )(a, w_t)


def _proj_kernel(a_ref, wt_ref, cos_ref, sin_ref, o_ref, wbuf, *scratch, rope_tiles, scaled_tiles, dil):
    j = pl.program_id(0)

    @pl.when(pl.program_id(1) == 0)
    def _():
        for c0 in range(0, wbuf.shape[0], LANES):
            wbuf[c0:c0 + LANES, :] = wt_ref[c0:c0 + LANES, :].astype(BF16)

    def any_of(tiles):
        hit = j == tiles[0]
        for t in tiles[1:]:
            hit = jnp.logical_or(hit, j == t)
        return hit

    def body(rope):
        if rope:
            factor = jnp.where(any_of(scaled_tiles), QK_SCALE, 1.0)
            cosf = cos_ref[...] * factor
            sinf = sin_ref[...] * factor
        res = _dot_nt(a_ref[...], wbuf[...])
        for c in range(res.shape[1] // LANES):
            cs = slice(c * LANES, (c + 1) * LANES)
            blk = res[:, cs]
            if rope:
                blk = blk * cosf + pltpu.roll(blk, DIL_HEAD_DIM // 2, 1) * sinf
            if dil == 1:
                o_ref[:, cs] = blk.astype(o_ref.dtype)
            else:
                acc = scratch[0]
                acc[c] = blk
                for r in range(dil):
                    o_ref[r, :, cs] = acc[c, pl.ds(r, o_ref.shape[1], stride=dil), :].astype(o_ref.dtype)

    is_rope = any_of(rope_tiles)
    pl.when(is_rope)(functools.partial(body, True))
    pl.when(jnp.logical_not(is_rope))(functools.partial(body, False))


def _projection(a, w_t, layer, cosf, sinf, *, n_tiles, src_col, rope_tiles, scaled_tiles, dil, name):
    m, k = a.shape
    s = cosf.shape[0]
    tn = PROJ_TILE
    n = n_tiles * tn
    tm = PROJ_TILE if dil == 1 else min(PROJ_TILE, CHUNK * dil)
    tab = pl.BlockSpec((tm, DIL_HEAD_DIM), lambda j, i: (i % (s // tm), 0))
    if dil == 1:
        out_spec = pl.BlockSpec((tm, tn), lambda j, i: (i, j))
        out_shape = jax.ShapeDtypeStruct((m, n), BF16)
        scratch = []
    else:
        out_spec = pl.BlockSpec((None, dil, tm // dil, tn), lambda j, i: (i, 0, 0, j))
        out_shape = jax.ShapeDtypeStruct((m // tm, dil, tm // dil, n), BF16)
        scratch = [pltpu.VMEM((tn // LANES, tm, LANES), F32)]
    return pl.pallas_call(
        functools.partial(_proj_kernel, rope_tiles=rope_tiles, scaled_tiles=scaled_tiles, dil=dil),
        grid=(n_tiles, m // tm),
        in_specs=[pl.BlockSpec((tm, k), lambda j, i: (i, 0)),
                  pl.BlockSpec((None, pl.Element(tn), pl.Element(k)), lambda j, i: (layer, src_col(j), 0)),
                  tab, tab],
        out_specs=out_spec,
        out_shape=out_shape,
        scratch_shapes=[pltpu.VMEM((tn, k), BF16)] + scratch,
        compiler_params=_cparams(("parallel", "arbitrary")),
        name=name,
    )(a, w_t, cosf, sinf)


def _ssd_kernel(z_ref, xs_ref, bc_ref, dt_ref, cwx_ref, cwbc_ref, cbx_ref, cbbc_ref,
                dtb_ref, alog_ref, dskip_ref, ng_ref, o_ref, xbuf, bcbuf, state, ybuf):
    c = pl.program_id(1)

    @pl.when(c == 0)
    def _():
        xbuf[CHUNK:, :] = jnp.zeros((CHUNK, SSD_INNER), BF16)
        bcbuf[CHUNK:, :] = jnp.zeros((CHUNK, 2 * SSD_BC), BF16)
        state[...] = jnp.zeros_like(state)

    srow = lax.broadcasted_iota(jnp.int32, (CHUNK, 2 * CHUNK), 0)
    scol = lax.broadcasted_iota(jnp.int32, (CHUNK, 2 * CHUNK), 1)
    shifts = [jnp.where(scol == ((srow - back) & (2 * CHUNK - 1)), 1.0, 0.0).astype(BF16)
              for back in range(1, SSD_CONV)]

    def conv_silu(in_ref, buf, w_ref, b_ref):
        cur = in_ref[...]
        buf[0:CHUNK, :] = cur
        ext = buf[...]
        acc = b_ref[...] + w_ref[SSD_CONV - 1:SSD_CONV, :] * cur.astype(F32)
        for back in range(1, SSD_CONV):
            tap = SSD_CONV - 1 - back
            acc = acc + w_ref[tap:tap + 1, :] * _dot(shifts[back - 1], ext)
        buf[CHUNK:, :] = cur
        return _silu(acc)

    xs = conv_silu(xs_ref, xbuf, cwx_ref, cbx_ref)
    bc = conv_silu(bc_ref, bcbuf, cwbc_ref, cbbc_ref)
    xs_b = xs.astype(BF16)

    v = dt_ref[...] + dtb_ref[...]
    dt = jnp.maximum(v, 0.0) + jnp.log1p(jnp.exp(-jnp.abs(v)))
    a_neg = -jnp.exp(alog_ref[...])
    da = dt * a_neg
    row = lax.broadcasted_iota(jnp.int32, (CHUNK, CHUNK), 0)
    col = lax.broadcasted_iota(jnp.int32, (CHUNK, CHUNK), 1)
    causal = row >= col
    tril = jnp.where(causal, 1.0, 0.0).astype(F32)
    acs = jnp.dot(tril, da, preferred_element_type=F32, precision=lax.Precision.HIGHEST)
    acs2 = acs * LOG2E
    ldt = jnp.log2(dt)
    arow_t = (acs2 - ldt).T
    total2 = acs2[CHUNK - 1:CHUNK, :]
    w_t = jnp.exp2(total2 - acs2 + ldt).T
    lane_lo = lax.broadcasted_iota(jnp.int32, (CHUNK, LANES), 1) < SSD_HEAD_DIM

    for g in range(SSD_GROUPS):
        bm_g = bc[:, g * SSD_STATE:(g + 1) * SSD_STATE]
        cm_g = bc[:, SSD_BC + g * SSD_STATE:SSD_BC + (g + 1) * SSD_STATE]
        cb = _dot_nt(cm_g.astype(BF16), bm_g.astype(BF16))
        bm_gt = bm_g.T
        for pair in range(SSD_REP // 2):
            h0 = g * SSD_REP + 2 * pair
            ps = slice(h0 * SSD_HEAD_DIM, (h0 + 2) * SSD_HEAD_DIM)
            xs_p = xs_b[:, ps]
            st_old = state[h0 // 2]
            rhs = jnp.concatenate([xs_p, st_old.astype(BF16)], axis=0)
            ys, sts, cds = [], [], []
            for h in (h0, h0 + 1):
                a_col = jnp.broadcast_to(acs2[:, h:h + 1], (CHUNK, CHUNK))
                decay = jnp.exp2(jnp.where(causal, a_col - arow_t[h:h + 1, :], -jnp.inf))
                e_col = jnp.exp2(a_col)
                lhs = jnp.concatenate([(cb * decay).astype(BF16), (cm_g * e_col).astype(BF16)], axis=1)
                ys.append(_dot(lhs, rhs))
                sts.append(_dot((bm_gt * w_t[h:h + 1, :]).astype(BF16), xs_p))
                cds.append(e_col[CHUNK - 1:CHUNK, :])
            ybuf[:, ps] = jnp.where(lane_lo, ys[0], ys[1])
            chunk_decay = jnp.where(lane_lo[0:1, :], cds[0], cds[1])
            state[h0 // 2] = st_old * chunk_decay + jnp.where(lane_lo, sts[0], sts[1])

    y = (ybuf[...] + xs * dskip_ref[...]) * _silu(z_ref[...].astype(F32))
    r = lax.rsqrt(jnp.mean(y * y, axis=-1, keepdims=True) + NORM_EPS)
    o_ref[...] = (y * r * ng_ref[...]).astype(o_ref.dtype)


def _ssd_branch(proj, dt_raw, conv_w, conv_b, dt_bias, a_log, d_skip, norm_g):
    b, s, _ = proj.shape
    nc = s // CHUNK

    def col(width, tile):
        return pl.BlockSpec((None, CHUNK, width), lambda bi, ci: (bi, ci, tile * PROJ_TILE // width))

    def const(shape):
        return pl.BlockSpec(shape, lambda bi, ci: (0,) * len(shape))

    pad = LANES - SSD_HEADS
    dtb = jnp.pad(dt_bias, (0, pad))[None, :]
    alog = jnp.pad(a_log, (0, pad))[None, :]
    dskip = jnp.repeat(d_skip, SSD_HEAD_DIM)[None, :]
    return pl.pallas_call(
        _ssd_kernel,
        grid=(b, nc),
        in_specs=[col(SSD_INNER, T_Z), col(SSD_INNER, T_XS), col(2 * SSD_BC, T_BC),
                  pl.BlockSpec((None, CHUNK, LANES), lambda bi, ci: (bi, ci, 0)),
                  const((SSD_CONV, SSD_INNER)), const((SSD_CONV, 2 * SSD_BC)),
                  const((1, SSD_INNER)), const((1, 2 * SSD_BC)),
                  const((1, LANES)), const((1, LANES)), const((1, SSD_INNER)), const((1, SSD_INNER))],
        out_specs=pl.BlockSpec((None, CHUNK, SSD_INNER), lambda bi, ci: (bi, ci, 0)),
        out_shape=jax.ShapeDtypeStruct((b, s, SSD_INNER), BF16),
        scratch_shapes=[pltpu.VMEM((2 * CHUNK, SSD_INNER), BF16),
                        pltpu.VMEM((2 * CHUNK, 2 * SSD_BC), BF16),
                        pltpu.VMEM((SSD_HEADS // 2, SSD_STATE, 2 * SSD_HEAD_DIM), F32),
                        pltpu.VMEM((CHUNK, SSD_INNER), F32)],
        compiler_params=_cparams(("parallel", "arbitrary")),
        name="ssd_mixer",
    )(proj, proj, proj, dt_raw, conv_w[:, :SSD_INNER], conv_w[:, SSD_INNER:],
      conv_b[None, :SSD_INNER], conv_b[None, SSD_INNER:], dtb, alog, dskip, norm_g[None, :])


def _rope_tables(s):
    inv_freq = ROPE_THETA ** (-jnp.arange(0, RET_QK_DIM, 2, dtype=F32) / RET_QK_DIM)
    ang = jnp.arange(s, dtype=F32)[:, None] * inv_freq[None, :]
    cos, sin = jnp.cos(ang), jnp.sin(ang)
    return jnp.concatenate([cos, cos], axis=-1), jnp.concatenate([-sin, sin], axis=-1)


_LOG_GAMMA = [float(np.log(np.float32(1.0) - np.exp2(np.float32(-5.0 - h)))) for h in range(RET_HEADS)]


def _ret_kernel(q_ref, k_ref, v_lo_ref, v_hi_ref, g_lo_ref, g_hi_ref, o_ref, state, dmask, qdec, kdec):
    c = pl.program_id(1)

    @pl.when(c == 0)
    def _():
        state[...] = jnp.zeros_like(state)
        li = lax.broadcasted_iota(jnp.int32, (CHUNK, CHUNK), 0).astype(F32)
        si = lax.broadcasted_iota(jnp.int32, (CHUNK, CHUNK), 1).astype(F32)
        rel = li - si
        for h in range(RET_HEADS):
            lg = _LOG_GAMMA[h]
            dmask[h] = jnp.where(rel >= 0, jnp.exp(jnp.maximum(rel, 0.0) * lg), 0.0)
            qdec[h] = jnp.exp((li + 1.0) * lg)
            kdec[h] = jnp.exp((CHUNK - 1.0 - li) * lg)

    for h in range(RET_HEADS):
        qs = slice(h * RET_QK_DIM, (h + 1) * RET_QK_DIM)
        vs = slice(h * RET_V_DIM, (h + 1) * RET_V_DIM)
        half = RET_HEADS // 2
        v_ref, g_ref = (v_lo_ref, g_lo_ref) if h < half else (v_hi_ref, g_hi_ref)
        hv = slice((h % half) * RET_V_DIM, (h % half + 1) * RET_V_DIM)
        qr = q_ref[:, qs]
        kr = k_ref[:, qs]
        v_h = v_ref[:, hv]
        scores = _dot_nt(qr, kr) * dmask[h]
        inner = _dot(scores.astype(BF16), v_h)
        st_old = state[h]
        cross = _dot((qr.astype(F32) * qdec[h]).astype(BF16), st_old.astype(BF16))
        kv = _dot((kr.astype(F32) * kdec[h]).T.astype(BF16), v_h)
        state[h] = st_old * math.exp(CHUNK * _LOG_GAMMA[h]) + kv
        o = inner + cross
        o = o * lax.rsqrt(jnp.mean(o * o, axis=-1, keepdims=True) + NORM_EPS)
        o_ref[:, vs] = (_silu(g_ref[:, hv].astype(F32)) * o).astype(o_ref.dtype)


def _ret_branch(proj):
    b, s, _ = proj.shape
    nc = s // CHUNK

    def col(tile):
        return pl.BlockSpec((None, CHUNK, PROJ_TILE), lambda bi, ci: (bi, ci, tile))

    return pl.pallas_call(
        _ret_kernel,
        grid=(b, nc),
        in_specs=[col(T_RQ), col(T_RK), col(T_RV), col(T_RV + 1), col(T_RG), col(T_RG + 1)],
        out_specs=pl.BlockSpec((None, CHUNK, RET_V), lambda bi, ci: (bi, ci, 0)),
        out_shape=jax.ShapeDtypeStruct((b, s, RET_V), BF16),
        scratch_shapes=[pltpu.VMEM((RET_HEADS, RET_QK_DIM, RET_V_DIM), F32),
                        pltpu.VMEM((RET_HEADS, CHUNK, CHUNK), F32),
                        pltpu.VMEM((RET_HEADS, CHUNK, CHUNK), F32),
                        pltpu.VMEM((RET_HEADS, CHUNK, CHUNK), F32)],
        compiler_params=_cparams(("parallel", "arbitrary")),
        name="retention_mixer",
    )(proj, proj, proj, proj, proj, proj)


def _dil_kernel(q_ref, k_ref, v_ref, o_ref, lse_ref, kbuf, vbuf):
    n = pl.program_id(2)
    ext = 2 * DIL_HEAD_DIM

    @pl.when(n == 0)
    def _():
        kbuf[...] = jnp.zeros_like(kbuf)
        for h in range(DIL_HEADS):
            vbuf[:, h * ext:h * ext + DIL_HEAD_DIM] = jnp.zeros((2 * CHUNK, DIL_HEAD_DIM), BF16)
            vbuf[:, h * ext + DIL_HEAD_DIM:(h + 1) * ext] = jnp.ones((2 * CHUNK, DIL_HEAD_DIM), BF16)

    parity = n % 2
    off = pl.multiple_of(parity * CHUNK, CHUNK)
    parts, part_rows = q_ref.shape[0], q_ref.shape[1]
    for p in range(parts):
        dst = pl.ds(off + p * part_rows, part_rows)
        kbuf[dst, :] = k_ref[p]
        for h in range(DIL_HEADS):
            vbuf[dst, h * ext:h * ext + DIL_HEAD_DIM] = v_ref[p, :, h * DIL_HEAD_DIM:(h + 1) * DIL_HEAD_DIM]

    def q_head(cols):
        return jnp.concatenate([q_ref[p, :, cols] for p in range(parts)], axis=0)

    qi = lax.broadcasted_iota(jnp.int32, (CHUNK, 2 * CHUNK), 0)
    kj = lax.broadcasted_iota(jnp.int32, (CHUNK, 2 * CHUNK), 1)
    kl = kj & (CHUNK - 1)
    is_cur = (kj >= CHUNK) == (parity == 1)
    mask = jnp.logical_or(jnp.logical_and(is_cur, kl <= qi),
                          jnp.logical_and(jnp.logical_not(is_cur), jnp.logical_and(kl >= qi, n > 0)))
    lane = lax.broadcasted_iota(jnp.int32, (CHUNK, LANES), 1)
    lse_all = jnp.zeros((CHUNK, LANES), F32)
    for h0 in range(0, DIL_HEADS, DIL_HEAD_GROUP):
        heads = range(h0, h0 + DIL_HEAD_GROUP)
        sl = {h: slice(h * DIL_HEAD_DIM, (h + 1) * DIL_HEAD_DIM) for h in heads}
        sc = {h: jnp.where(mask, _dot_nt(q_head(sl[h]), kbuf[:, sl[h]]), -jnp.inf) for h in heads}
        mx = {h: jnp.max(sc[h], axis=-1, keepdims=True) for h in heads}
        p = {h: jnp.exp(sc[h] - mx[h]).astype(BF16) for h in heads}
        res = {h: _dot(p[h], vbuf[:, h * ext:(h + 1) * ext]) for h in heads}
        for h in heads:
            den = res[h][:, DIL_HEAD_DIM:]
            o_h = (res[h][:, :DIL_HEAD_DIM] / den).astype(o_ref.dtype)
            for p in range(parts):
                o_ref[p, :, sl[h]] = o_h[p * part_rows:(p + 1) * part_rows]
            lse_all = jnp.where(lane == h, mx[h] + jnp.log(den), lse_all)
    for p in range(parts):
        lse_ref[p] = lse_all[p * part_rows:(p + 1) * part_rows]


def _dil_group(qkv, tiles_qkv, dil, batch):
    tiles, _, rows, _ = qkv.shape
    parts = CHUNK // rows
    nb = tiles // parts // batch

    def part(width, tile):
        return pl.BlockSpec((parts, None, rows, width), lambda bi, r, n: (bi * nb + n, r, 0, tile))

    return pl.pallas_call(
        _dil_kernel,
        grid=(batch, dil, nb),
        in_specs=[part(DIL_WIDTH, t) for t in tiles_qkv],
        out_specs=[part(DIL_WIDTH, 0), part(LANES, 0)],
        out_shape=[jax.ShapeDtypeStruct((tiles, dil, rows, DIL_WIDTH), BF16),
                   jax.ShapeDtypeStruct((tiles, dil, rows, LANES), F32)],
        scratch_shapes=[pltpu.VMEM((2 * CHUNK, DIL_WIDTH), BF16), pltpu.VMEM((2 * CHUNK, 2 * DIL_WIDTH), BF16)],
        compiler_params=_cparams(("parallel", "parallel", "arbitrary")),
        name=f"dilated_attention_d{dil}",
    )(qkv, qkv, qkv)


def _dil_combine_kernel(o0_ref, o1_ref, o2_ref, l0_ref, l1_ref, l2_ref, g_ref, y_ref, ob1, ob2, lb1, lb2):
    for (o_ref, l_ref, ob, lb, dil) in ((o1_ref, l1_ref, ob1, lb1, DIL_DILATIONS[1]),
                                        (o2_ref, l2_ref, ob2, lb2, DIL_DILATIONS[2])):
        rows = o_ref.shape[1]
        for r in range(dil):
            lb[pl.ds(r, rows, stride=dil), :] = l_ref[r]
            for h in range(DIL_HEADS):
                hs = slice(h * DIL_HEAD_DIM, (h + 1) * DIL_HEAD_DIM)
                ob[h, pl.ds(r, rows, stride=dil), :] = o_ref[r, :, hs].astype(F32)

    l0, l1, l2 = l0_ref[...], lb1[...], lb2[...]
    m = jnp.maximum(jnp.maximum(l0, l1), l2)
    e0, e1, e2 = jnp.exp(l0 - m), jnp.exp(l1 - m), jnp.exp(l2 - m)
    inv = 1.0 / (e0 + e1 + e2)
    w0, w1, w2 = e0 * inv, e1 * inv, e2 * inv
    rows = l0.shape[0]
    for h in range(DIL_HEADS):
        hs = slice(h * DIL_HEAD_DIM, (h + 1) * DIL_HEAD_DIM)

        def lane(w):
            return jnp.broadcast_to(w[:, h:h + 1], (rows, DIL_HEAD_DIM))

        o = lane(w0) * o0_ref[:, hs].astype(F32) + lane(w1) * ob1[h] + lane(w2) * ob2[h]
        y_ref[:, hs] = (_silu(g_ref[:, hs].astype(F32)) * o).astype(y_ref.dtype)


def _dil_combine(outs, lses, proj2d):
    m = proj2d.shape[0]
    d1, d2 = DIL_DILATIONS[1], DIL_DILATIONS[2]
    tm = CHUNK * d1
    sub = tm // d2
    per = outs[2].shape[2] // sub
    o_spec = pl.BlockSpec((tm, DIL_WIDTH), lambda i: (i, 0))
    l_spec = pl.BlockSpec((tm, LANES), lambda i: (i, 0))

    def grouped(width):
        return (pl.BlockSpec((None, d1, CHUNK, width), lambda i: (i, 0, 0, 0)),
                pl.BlockSpec((None, d2, sub, width), lambda i: (i // per, 0, i % per, 0)))

    o1_spec, o2_spec = grouped(DIL_WIDTH)
    l1_spec, l2_spec = grouped(LANES)
    return pl.pallas_call(
        _dil_combine_kernel,
        grid=(m // tm,),
        in_specs=[o_spec, o1_spec, o2_spec, l_spec, l1_spec, l2_spec,
                  pl.BlockSpec((tm, DIL_WIDTH), lambda i: (i, T_DG))],
        out_specs=o_spec,
        out_shape=jax.ShapeDtypeStruct((m, DIL_WIDTH), BF16),
        scratch_shapes=[pltpu.VMEM((DIL_HEADS, tm, DIL_HEAD_DIM), F32), pltpu.VMEM((DIL_HEADS, tm, DIL_HEAD_DIM), F32),
                        pltpu.VMEM((tm, LANES), F32), pltpu.VMEM((tm, LANES), F32)],
        compiler_params=_cparams(("parallel",)),
        name="dilated_combine",
    )(outs[0].reshape(m, DIL_WIDTH), outs[1], outs[2], lses[0].reshape(m, LANES), lses[1], lses[2], proj2d)


def _merge_kernel(ya_ref, yb_ref, yc_ref, wa_ref, wb_ref, wc_ref, ga_ref, gb_ref, gc_ref, o_ref):
    acc = _sigmoid(ga_ref[...].astype(F32)) * _dot(ya_ref[...], wa_ref[...])
    acc = acc + _sigmoid(gb_ref[...].astype(F32)) * _dot(yb_ref[...], wb_ref[...])
    acc = acc + _sigmoid(gc_ref[...].astype(F32)) * _dot(yc_ref[...], wc_ref[...])
    o_ref[...] = acc.astype(o_ref.dtype)


def _merge(ya, yb, yc, wa, wb, wc, proj2d, tm=512, tn=PROJ_TILE):
    m = ya.shape[0]
    d = wa.shape[1]

    def act(width):
        return pl.BlockSpec((tm, width), lambda j, i: (i, 0))

    def wgt(width):
        return pl.BlockSpec((width, tn), lambda j, i: (0, j))

    def gate(branch):
        return pl.BlockSpec((tm, tn), lambda j, i: (i, T_MG + branch * (d // tn) + j))

    return pl.pallas_call(
        _merge_kernel,
        grid=(d // tn, m // tm),
        in_specs=[act(SSD_INNER), act(RET_V), act(DIL_WIDTH), wgt(SSD_INNER), wgt(RET_V), wgt(DIL_WIDTH),
                  gate(0), gate(1), gate(2)],
        out_specs=pl.BlockSpec((tm, tn), lambda j, i: (i, j)),
        out_shape=jax.ShapeDtypeStruct((m, d), BF16),
        compiler_params=_cparams(("parallel", "parallel")),
        name="gated_merge",
    )(ya, yb, yc, wa, wb, wc, proj2d, proj2d, proj2d)


def _out_kernel(a_ref, w_ref, x_ref, g_ref, xo_ref, ho_ref):
    x = x_ref[...] + _dot(a_ref[...], w_ref[...])
    xo_ref[...] = x
    r = lax.rsqrt(jnp.mean(x * x, axis=-1, keepdims=True) + NORM_EPS)
    ho_ref[...] = (x * r * g_ref[...]).astype(ho_ref.dtype)


def _out_proj(merged, w_out, x2d, g_next, norm_dtype, tm=512):
    m, d = x2d.shape
    row = pl.BlockSpec((tm, d), lambda i: (i, 0))
    return pl.pallas_call(
        _out_kernel,
        grid=(m // tm,),
        in_specs=[row, pl.BlockSpec((d, d), lambda i: (0, 0)), row, pl.BlockSpec((1, d), lambda i: (0, 0))],
        out_specs=[row, row],
        out_shape=[jax.ShapeDtypeStruct((m, d), F32), jax.ShapeDtypeStruct((m, d), norm_dtype)],
        compiler_params=_cparams(("parallel",)),
        name="out_proj_residual_norm",
    )(merged, w_out, x2d, g_next.reshape(1, d))


def kernel(x, norm_g, w_in, conv_w, conv_b, dt_bias, a_log, d_skip, ssd_norm_g,
           w_o_ssd, w_o_ret, w_o_dil, w_out, final_norm_g):
    b, s, d = x.shape
    depth = w_in.shape[0]
    m = b * s
    assert s % (max(DIL_DILATIONS) * CHUNK) == 0 and d == 2048
    cosf, sinf = _rope_tables(s)
    w_t = jnp.swapaxes(w_in, 1, 2)

    x2d = x.reshape(m, d)
    h = _rms_norm(x2d, norm_g[0], BF16)
    for layer in range(depth):
        proj2d = _projection(h, w_t, layer, cosf, sinf, n_tiles=N_MAIN_TILES, src_col=_main_src_col,
                             rope_tiles=(T_RQ, T_RK, T_DQ, T_DK), scaled_tiles=(T_RK, T_DQ), dil=1,
                             name="in_proj")
        dt_raw = _dt_projection(h, w_t, layer)
        proj = proj2d.reshape(b, s, N_MAIN)

        y_a = _ssd_branch(proj, dt_raw.reshape(b, s, LANES), conv_w[layer], conv_b[layer], dt_bias[layer],
                          a_log[layer], d_skip[layer], ssd_norm_g[layer])
        y_b = _ret_branch(proj)

        outs, lses = [], []
        for gi, dil in enumerate(DIL_DILATIONS):
            if dil == 1:
                qkv, tiles_qkv = proj2d.reshape(m // CHUNK, 1, CHUNK, N_MAIN), (T_DQ, T_DK, T_DV)
            else:
                first = W_DQ1_COL if gi == 1 else W_DQ2_COL
                qkv = _projection(h, w_t, layer, cosf, sinf, n_tiles=3,
                                  src_col=lambda j, f=first: (f // 8 + j * (PROJ_TILE // 8)) * 8,
                                  rope_tiles=(0, 1), scaled_tiles=(0,), dil=dil, name=f"dqkv_proj_d{dil}")
                tiles_qkv = (0, 1, 2)
            o, lse = _dil_group(qkv, tiles_qkv, dil, b)
            outs.append(o)
            lses.append(lse)
        y_c = _dil_combine(outs, lses, proj2d)

        merged = _merge(y_a.reshape(m, SSD_INNER), y_b.reshape(m, RET_V), y_c,
                        w_o_ssd[layer].astype(BF16), w_o_ret[layer].astype(BF16), w_o_dil[layer].astype(BF16),
                        proj2d)
        last = layer == depth - 1
        g_next = final_norm_g if last else norm_g[layer + 1]
        x2d, h = _out_proj(merged, w_out[layer].astype(BF16), x2d, g_next, F32 if last else BF16)
    return h.reshape(b, s, d)
```

```python
import functools
import math

import numpy as np
import jax
import jax.numpy as jnp
from jax import lax
from jax.experimental import pallas as pl
from jax.experimental.pallas import tpu as pltpu

F32 = jnp.float32
BF16 = jnp.bfloat16

NORM_EPS = 1e-6
CHUNK = 128
ROPE_THETA = 10000.0
LOG2E = 1.4426950408889634

SSD_INNER = 2048
SSD_HEAD_DIM = 64
SSD_HEADS = 32
SSD_GROUPS = 4
SSD_REP = SSD_HEADS // SSD_GROUPS
SSD_STATE = 128
SSD_CONV = 4
SSD_BC = SSD_GROUPS * SSD_STATE

RET_HEADS = 8
RET_QK_DIM = 128
RET_V_DIM = 256
RET_QK = RET_HEADS * RET_QK_DIM
RET_V = RET_HEADS * RET_V_DIM

DIL_DILATIONS = (1, 4, 16)
DIL_HEADS = 8
DIL_HEAD_DIM = 128
DIL_WIDTH = DIL_HEADS * DIL_HEAD_DIM
DIL_HEAD_GROUP = 4
DIL_BLOCKS = 2
QK_SCALE = DIL_HEAD_DIM ** -0.5

LANES = 128
MXU_COLS = 256
VMEM_LIMIT = 56 * 1024 * 1024
PROJ_TILE = 1024

W_DT_COL = 5120
W_AFTER_DT = W_DT_COL + SSD_HEADS
W_DQ1_COL = W_AFTER_DT + 9 * PROJ_TILE
W_DQ2_COL = W_AFTER_DT + 12 * PROJ_TILE

T_Z, T_XS, T_BC, T_RQ, T_RK, T_RV, T_RG, T_DQ, T_DK, T_DV, T_DG, T_MG = 0, 2, 4, 5, 6, 7, 9, 11, 12, 13, 14, 15
N_MAIN_TILES = 21
N_MAIN = N_MAIN_TILES * PROJ_TILE


def _main_src_col(j):
    sub = 8
    return (j * (PROJ_TILE // sub) + jnp.where(j < T_RQ, 0, SSD_HEADS // sub)
            + jnp.where(j < T_DG, 0, 6 * PROJ_TILE // sub)) * sub


def _cparams(sem):
    return pltpu.CompilerParams(dimension_semantics=sem, vmem_limit_bytes=VMEM_LIMIT)


def _sigmoid(v):
    return 1.0 / (1.0 + jnp.exp(-v))


def _silu(v):
    half = 0.5 * v
    return half + half * jnp.tanh(half)


def _dot(a, b):
    return jnp.dot(a, b, preferred_element_type=F32)


def _dot_nt(a, b):
    return lax.dot_general(a, b, (((1,), (1,)), ((), ())), preferred_element_type=F32)


def _norm_kernel(x_ref, g_ref, o_ref):
    x = x_ref[...]
    r = lax.rsqrt(jnp.mean(x * x, axis=-1, keepdims=True) + NORM_EPS)
    o_ref[...] = (x * r * g_ref[...]).astype(o_ref.dtype)


def _rms_norm(x2d, g, out_dtype, tm=512):
    m, d = x2d.shape
    return pl.pallas_call(
        _norm_kernel,
        grid=(m // tm,),
        in_specs=[pl.BlockSpec((tm, d), lambda i: (i, 0)),
                  pl.BlockSpec((1, d), lambda i: (0, 0))],
        out_specs=pl.BlockSpec((tm, d), lambda i: (i, 0)),
        out_shape=jax.ShapeDtypeStruct((m, d), out_dtype),
        compiler_params=_cparams(("parallel",)),
        name="rms_norm",
    )(x2d, g.reshape(1, d))


def _dt_kernel(a_ref, wt_ref, o_ref):
    res = _dot_nt(a_ref[...], wt_ref[...].astype(BF16))
    lane = lax.broadcasted_iota(jnp.int32, res.shape, 1)
    o_ref[...] = jnp.where(lane < SSD_HEADS, res, 0.0)


def _dt_projection(a, w_t, layer, tm=PROJ_TILE):
    m, k = a.shape
    return pl.pallas_call(
        _dt_kernel,
        grid=(m // tm,),
        in_specs=[pl.BlockSpec((tm, k), lambda i: (i, 0)),
                  pl.BlockSpec((None, LANES, k), lambda i: (layer, W_DT_COL // LANES, 0))],
        out_specs=pl.BlockSpec((tm, LANES), lambda i: (i, 0)),
        out_shape=jax.ShapeDtypeStruct((m, LANES), F32),
        compiler_params=_cparams(("parallel",)),
        name="dt_proj",
    )(a, w_t)


def _proj_kernel(a_ref, wt_ref, cos_ref, sin_ref, o_ref, wbuf, *scratch, rope_tiles, scaled_tiles, dil):
    j = pl.program_id(0)

    @pl.when(pl.program_id(1) == 0)
    def _():
        for c0 in range(0, wbuf.shape[0], LANES):
            wbuf[c0:c0 + LANES, :] = wt_ref[c0:c0 + LANES, :].astype(BF16)

    def any_of(tiles):
        hit = j == tiles[0]
        for t in tiles[1:]:
            hit = jnp.logical_or(hit, j == t)
        return hit

    def body(rope):
        if rope:
            factor = jnp.where(any_of(scaled_tiles), QK_SCALE, 1.0)
            cosf = cos_ref[...] * factor
            sinf = sin_ref[...] * factor
        res = _dot_nt(a_ref[...], wbuf[...])
        for c in range(res.shape[1] // LANES):
            cs = slice(c * LANES, (c + 1) * LANES)
            blk = res[:, cs]
            if rope:
                blk = blk * cosf + pltpu.roll(blk, DIL_HEAD_DIM // 2, 1) * sinf
            if dil == 1:
                o_ref[:, cs] = blk.astype(o_ref.dtype)
            else:
                acc = scratch[0]
                acc[c] = blk
                for r in range(dil):
                    o_ref[r, :, cs] = acc[c, pl.ds(r, o_ref.shape[1], stride=dil), :].astype(o_ref.dtype)

    is_rope = any_of(rope_tiles)
    pl.when(is_rope)(functools.partial(body, True))
    pl.when(jnp.logical_not(is_rope))(functools.partial(body, False))


def _projection(a, w_t, layer, cosf, sinf, *, n_tiles, src_col, rope_tiles, scaled_tiles, dil, name):
    m, k = a.shape
    s = cosf.shape[0]
    tn = PROJ_TILE
    n = n_tiles * tn
    tm = PROJ_TILE if dil == 1 else min(PROJ_TILE, CHUNK * dil)
    tab = pl.BlockSpec((tm, DIL_HEAD_DIM), lambda j, i: (i % (s // tm), 0))
    if dil == 1:
        out_spec = pl.BlockSpec((tm, tn), lambda j, i: (i, j))
        out_shape = jax.ShapeDtypeStruct((m, n), BF16)
        scratch = []
    else:
        out_spec = pl.BlockSpec((None, dil, tm // dil, tn), lambda j, i: (i, 0, 0, j))
        out_shape = jax.ShapeDtypeStruct((m // tm, dil, tm // dil, n), BF16)
        scratch = [pltpu.VMEM((tn // LANES, tm, LANES), F32)]
    return pl.pallas_call(
        functools.partial(_proj_kernel, rope_tiles=rope_tiles, scaled_tiles=scaled_tiles, dil=dil),
        grid=(n_tiles, m // tm),
        in_specs=[pl.BlockSpec((tm, k), lambda j, i: (i, 0)),
                  pl.BlockSpec((None, pl.Element(tn), pl.Element(k)), lambda j, i: (layer, src_col(j), 0)),
                  tab, tab],
        out_specs=out_spec,
        out_shape=out_shape,
        scratch_shapes=[pltpu.VMEM((tn, k), BF16)] + scratch,
        compiler_params=_cparams(("parallel", "arbitrary")),
        name=name,
    )(a, w_t, cosf, sinf)


def _ssd_kernel(z_ref, xs_ref, bc_ref, dt_ref, cwx_ref, cwbc_ref, cbx_ref, cbbc_ref,
                dtb_ref, alog_ref, dskip_ref, ng_ref, o_ref, xbuf, bcbuf, state, ybuf):
    c = pl.program_id(1)

    @pl.when(c == 0)
    def _():
        xbuf[CHUNK:, :] = jnp.zeros((CHUNK, SSD_INNER), BF16)
        bcbuf[CHUNK:, :] = jnp.zeros((CHUNK, 2 * SSD_BC), BF16)
        state[...] = jnp.zeros_like(state)

    srow = lax.broadcasted_iota(jnp.int32, (CHUNK, 2 * CHUNK), 0)
    scol = lax.broadcasted_iota(jnp.int32, (CHUNK, 2 * CHUNK), 1)
    shifts = [jnp.where(scol == ((srow - back) & (2 * CHUNK - 1)), 1.0, 0.0).astype(BF16)
              for back in range(1, SSD_CONV)]

    def conv_silu(in_ref, buf, w_ref, b_ref):
        cur = in_ref[...]
        buf[0:CHUNK, :] = cur
        ext = buf[...]
        acc = b_ref[...] + w_ref[SSD_CONV - 1:SSD_CONV, :] * cur.astype(F32)
        for back in range(1, SSD_CONV):
            tap = SSD_CONV - 1 - back
            acc = acc + w_ref[tap:tap + 1, :] * _dot(shifts[back - 1], ext)
        buf[CHUNK:, :] = cur
        return _silu(acc)

    xs = conv_silu(xs_ref, xbuf, cwx_ref, cbx_ref)
    bc = conv_silu(bc_ref, bcbuf, cwbc_ref, cbbc_ref)
    xs_b = xs.astype(BF16)

    v = dt_ref[...] + dtb_ref[...]
    dt = jnp.maximum(v, 0.0) + jnp.log1p(jnp.exp(-jnp.abs(v)))
    a_neg = -jnp.exp(alog_ref[...])
    da = dt * a_neg
    row = lax.broadcasted_iota(jnp.int32, (CHUNK, CHUNK), 0)
    col = lax.broadcasted_iota(jnp.int32, (CHUNK, CHUNK), 1)
    causal = row >= col
    tril = jnp.where(causal, 1.0, 0.0).astype(F32)
    acs = jnp.dot(tril, da, preferred_element_type=F32, precision=lax.Precision.HIGHEST)
    acs2 = acs * LOG2E
    ldt = jnp.log2(dt)
    arow_t = (acs2 - ldt).T
    total2 = acs2[CHUNK - 1:CHUNK, :]
    w_t = jnp.exp2(total2 - acs2 + ldt).T
    lane_lo = lax.broadcasted_iota(jnp.int32, (CHUNK, LANES), 1) < SSD_HEAD_DIM

    for g in range(SSD_GROUPS):
        bm_g = bc[:, g * SSD_STATE:(g + 1) * SSD_STATE]
        cm_g = bc[:, SSD_BC + g * SSD_STATE:SSD_BC + (g + 1) * SSD_STATE]
        cb = _dot_nt(cm_g.astype(BF16), bm_g.astype(BF16))
        bm_gt = bm_g.T
        for pair in range(SSD_REP // 2):
            h0 = g * SSD_REP + 2 * pair
            ps = slice(h0 * SSD_HEAD_DIM, (h0 + 2) * SSD_HEAD_DIM)
            xs_p = xs_b[:, ps]
            st_old = state[h0 // 2]
            rhs = jnp.concatenate([xs_p, st_old.astype(BF16)], axis=0)
            ys, sts, cds = [], [], []
            for h in (h0, h0 + 1):
                a_col = jnp.broadcast_to(acs2[:, h:h + 1], (CHUNK, CHUNK))
                decay = jnp.exp2(jnp.where(causal, a_col - arow_t[h:h + 1, :], -jnp.inf))
                e_col = jnp.exp2(a_col)
                lhs = jnp.concatenate([(cb * decay).astype(BF16), (cm_g * e_col).astype(BF16)], axis=1)
                ys.append(_dot(lhs, rhs))
                sts.append(_dot((bm_gt * w_t[h:h + 1, :]).astype(BF16), xs_p))
                cds.append(e_col[CHUNK - 1:CHUNK, :])
            ybuf[:, ps] = jnp.where(lane_lo, ys[0], ys[1])
            chunk_decay = jnp.where(lane_lo[0:1, :], cds[0], cds[1])
            state[h0 // 2] = st_old * chunk_decay + jnp.where(lane_lo, sts[0], sts[1])

    y = (ybuf[...] + xs * dskip_ref[...]) * _silu(z_ref[...].astype(F32))
    r = lax.rsqrt(jnp.mean(y * y, axis=-1, keepdims=True) + NORM_EPS)
    o_ref[...] = (y * r * ng_ref[...]).astype(o_ref.dtype)


def _ssd_branch(proj, dt_raw, conv_w, conv_b, dt_bias, a_log, d_skip, norm_g):
    b, s, _ = proj.shape
    nc = s // CHUNK

    def col(width, tile):
        return pl.BlockSpec((None, CHUNK, width), lambda bi, ci: (bi, ci, tile * PROJ_TILE // width))

    def const(shape):
        return pl.BlockSpec(shape, lambda bi, ci: (0,) * len(shape))

    pad = LANES - SSD_HEADS
    dtb = jnp.pad(dt_bias, (0, pad))[None, :]
    alog = jnp.pad(a_log, (0, pad))[None, :]
    dskip = jnp.repeat(d_skip, SSD_HEAD_DIM)[None, :]
    return pl.pallas_call(
        _ssd_kernel,
        grid=(b, nc),
        in_specs=[col(SSD_INNER, T_Z), col(SSD_INNER, T_XS), col(2 * SSD_BC, T_BC),
                  pl.BlockSpec((None, CHUNK, LANES), lambda bi, ci: (bi, ci, 0)),
                  const((SSD_CONV, SSD_INNER)), const((SSD_CONV, 2 * SSD_BC)),
                  const((1, SSD_INNER)), const((1, 2 * SSD_BC)),
                  const((1, LANES)), const((1, LANES)), const((1, SSD_INNER)), const((1, SSD_INNER))],
        out_specs=pl.BlockSpec((None, CHUNK, SSD_INNER), lambda bi, ci: (bi, ci, 0)),
        out_shape=jax.ShapeDtypeStruct((b, s, SSD_INNER), BF16),
        scratch_shapes=[pltpu.VMEM((2 * CHUNK, SSD_INNER), BF16),
                        pltpu.VMEM((2 * CHUNK, 2 * SSD_BC), BF16),
                        pltpu.VMEM((SSD_HEADS // 2, SSD_STATE, 2 * SSD_HEAD_DIM), F32),
                        pltpu.VMEM((CHUNK, SSD_INNER), F32)],
        compiler_params=_cparams(("parallel", "arbitrary")),
        name="ssd_mixer",
    )(proj, proj, proj, dt_raw, conv_w[:, :SSD_INNER], conv_w[:, SSD_INNER:],
      conv_b[None, :SSD_INNER], conv_b[None, SSD_INNER:], dtb, alog, dskip, norm_g[None, :])


def _rope_tables(s):
    inv_freq = ROPE_THETA ** (-jnp.arange(0, RET_QK_DIM, 2, dtype=F32) / RET_QK_DIM)
    ang = jnp.arange(s, dtype=F32)[:, None] * inv_freq[None, :]
    cos, sin = jnp.cos(ang), jnp.sin(ang)
    return jnp.concatenate([cos, cos], axis=-1), jnp.concatenate([-sin, sin], axis=-1)


_LOG_GAMMA = [float(np.log(np.float32(1.0) - np.exp2(np.float32(-5.0 - h)))) for h in range(RET_HEADS)]


def _ret_kernel(q_ref, k_ref, v_lo_ref, v_hi_ref, g_lo_ref, g_hi_ref, o_ref, state, dmask, qdec, kdec):
    c = pl.program_id(1)

    @pl.when(c == 0)
    def _():
        state[...] = jnp.zeros_like(state)
        li = lax.broadcasted_iota(jnp.int32, (CHUNK, CHUNK), 0).astype(F32)
        si = lax.broadcasted_iota(jnp.int32, (CHUNK, CHUNK), 1).astype(F32)
        rel = li - si
        for h in range(RET_HEADS):
            lg = _LOG_GAMMA[h]
            dmask[h] = jnp.where(rel >= 0, jnp.exp(jnp.maximum(rel, 0.0) * lg), 0.0)
            qdec[h] = jnp.exp((li + 1.0) * lg)
            kdec[h] = jnp.exp((CHUNK - 1.0 - li) * lg)

    for h in range(RET_HEADS):
        qs = slice(h * RET_QK_DIM, (h + 1) * RET_QK_DIM)
        vs = slice(h * RET_V_DIM, (h + 1) * RET_V_DIM)
        half = RET_HEADS // 2
        v_ref, g_ref = (v_lo_ref, g_lo_ref) if h < half else (v_hi_ref, g_hi_ref)
        hv = slice((h % half) * RET_V_DIM, (h % half + 1) * RET_V_DIM)
        qr = q_ref[:, qs]
        kr = k_ref[:, qs]
        v_h = v_ref[:, hv]
        scores = _dot_nt(qr, kr) * dmask[h]
        inner = _dot(scores.astype(BF16), v_h)
        st_old = state[h]
        cross = _dot((qr.astype(F32) * qdec[h]).astype(BF16), st_old.astype(BF16))
        kv = _dot((kr.astype(F32) * kdec[h]).T.astype(BF16), v_h)
        state[h] = st_old * math.exp(CHUNK * _LOG_GAMMA[h]) + kv
        o = inner + cross
        o = o * lax.rsqrt(jnp.mean(o * o, axis=-1, keepdims=True) + NORM_EPS)
        o_ref[:, vs] = (_silu(g_ref[:, hv].astype(F32)) * o).astype(o_ref.dtype)


def _ret_branch(proj):
    b, s, _ = proj.shape
    nc = s // CHUNK

    def col(tile):
        return pl.BlockSpec((None, CHUNK, PROJ_TILE), lambda bi, ci: (bi, ci, tile))

    return pl.pallas_call(
        _ret_kernel,
        grid=(b, nc),
        in_specs=[col(T_RQ), col(T_RK), col(T_RV), col(T_RV + 1), col(T_RG), col(T_RG + 1)],
        out_specs=pl.BlockSpec((None, CHUNK, RET_V), lambda bi, ci: (bi, ci, 0)),
        out_shape=jax.ShapeDtypeStruct((b, s, RET_V), BF16),
        scratch_shapes=[pltpu.VMEM((RET_HEADS, RET_QK_DIM, RET_V_DIM), F32),
                        pltpu.VMEM((RET_HEADS, CHUNK, CHUNK), F32),
                        pltpu.VMEM((RET_HEADS, CHUNK, CHUNK), F32),
                        pltpu.VMEM((RET_HEADS, CHUNK, CHUNK), F32)],
        compiler_params=_cparams(("parallel", "arbitrary")),
        name="retention_mixer",
    )(proj, proj, proj, proj, proj, proj)


def _dil_kernel(q_ref, k_ref, v_ref, o_ref, lse_ref, kbuf, vbuf):
    n = pl.program_id(2)
    ext = 2 * DIL_HEAD_DIM
    first = slice(0, CHUNK)
    last = slice(DIL_BLOCKS * CHUNK, (DIL_BLOCKS + 1) * CHUNK)

    @pl.when(n == 0)
    def _():
        kbuf[first, :] = jnp.zeros((CHUNK, DIL_WIDTH), BF16)
        for h in range(DIL_HEADS):
            vbuf[first, h * ext:h * ext + DIL_HEAD_DIM] = jnp.zeros((CHUNK, DIL_HEAD_DIM), BF16)
            vbuf[:, h * ext + DIL_HEAD_DIM:(h + 1) * ext] = jnp.ones(((DIL_BLOCKS + 1) * CHUNK, DIL_HEAD_DIM), BF16)

    parts, part_rows = q_ref.shape[0], q_ref.shape[1]
    per_block = parts // DIL_BLOCKS
    for pi in range(parts):
        dst = slice(CHUNK + pi * part_rows, CHUNK + (pi + 1) * part_rows)
        kbuf[dst, :] = k_ref[pi]
        for h in range(DIL_HEADS):
            vbuf[dst, h * ext:h * ext + DIL_HEAD_DIM] = v_ref[pi, :, h * DIL_HEAD_DIM:(h + 1) * DIL_HEAD_DIM]

    qi = lax.broadcasted_iota(jnp.int32, (CHUNK, 2 * CHUNK), 0)
    kj = lax.broadcasted_iota(jnp.int32, (CHUNK, 2 * CHUNK), 1)
    kl = kj & (CHUNK - 1)
    band = jnp.logical_or(jnp.logical_and(kj >= CHUNK, kl <= qi), jnp.logical_and(kj < CHUNK, kl >= qi))
    band_first = jnp.logical_and(band, jnp.logical_or(kj >= CHUNK, n > 0))
    lane = lax.broadcasted_iota(jnp.int32, (CHUNK, LANES), 1)

    for blk in range(DIL_BLOCKS):
        keys = slice(blk * CHUNK, (blk + 2) * CHUNK)
        mask = band_first if blk == 0 else band
        block_parts = range(blk * per_block, (blk + 1) * per_block)

        def q_head(cols):
            return jnp.concatenate([q_ref[pi, :, cols] for pi in block_parts], axis=0)

        lse_all = jnp.zeros((CHUNK, LANES), F32)
        for h0 in range(0, DIL_HEADS, DIL_HEAD_GROUP):
            heads = range(h0, h0 + DIL_HEAD_GROUP)
            sl = {h: slice(h * DIL_HEAD_DIM, (h + 1) * DIL_HEAD_DIM) for h in heads}
            sc = {h: jnp.where(mask, _dot_nt(q_head(sl[h]), kbuf[keys, sl[h]]), -jnp.inf) for h in heads}
            mx = {h: jnp.max(sc[h], axis=-1, keepdims=True) for h in heads}
            pr = {h: jnp.exp(sc[h] - mx[h]).astype(BF16) for h in heads}
            res = {h: _dot(pr[h], vbuf[keys, h * ext:(h + 1) * ext]) for h in heads}
            for h in heads:
                den = res[h][:, DIL_HEAD_DIM:]
                o_h = (res[h][:, :DIL_HEAD_DIM] / den).astype(o_ref.dtype)
                for i, pi in enumerate(block_parts):
                    o_ref[pi, :, sl[h]] = o_h[i * part_rows:(i + 1) * part_rows]
                lse_all = jnp.where(lane == h, mx[h] + jnp.log(den), lse_all)
        for i, pi in enumerate(block_parts):
            lse_ref[pi] = lse_all[i * part_rows:(i + 1) * part_rows]

    kbuf[first, :] = kbuf[last, :]
    for h in range(DIL_HEADS):
        vbuf[first, h * ext:h * ext + DIL_HEAD_DIM] = vbuf[last, h * ext:h * ext + DIL_HEAD_DIM]


def _dil_group(qkv, tiles_qkv, dil, batch):
    tiles, _, rows, _ = qkv.shape
    parts = DIL_BLOCKS * CHUNK // rows
    nb = tiles // parts // batch
    assert nb * parts * batch == tiles

    def part(width, tile):
        return pl.BlockSpec((parts, None, rows, width), lambda bi, r, n: (bi * nb + n, r, 0, tile))

    return pl.pallas_call(
        _dil_kernel,
        grid=(batch, dil, nb),
        in_specs=[part(DIL_WIDTH, t) for t in tiles_qkv],
        out_specs=[part(DIL_WIDTH, 0), part(LANES, 0)],
        out_shape=[jax.ShapeDtypeStruct((tiles, dil, rows, DIL_WIDTH), BF16),
                   jax.ShapeDtypeStruct((tiles, dil, rows, LANES), F32)],
        scratch_shapes=[pltpu.VMEM(((DIL_BLOCKS + 1) * CHUNK, DIL_WIDTH), BF16),
                        pltpu.VMEM(((DIL_BLOCKS + 1) * CHUNK, 2 * DIL_WIDTH), BF16)],
        compiler_params=_cparams(("parallel", "parallel", "arbitrary")),
        name=f"dilated_attention_d{dil}",
    )(qkv, qkv, qkv)


def _dil_combine_kernel(o0_ref, o1_ref, o2_ref, l0_ref, l1_ref, l2_ref, g_ref, y_ref, ob1, ob2, lb1, lb2):
    for (o_ref, l_ref, ob, lb, dil) in ((o1_ref, l1_ref, ob1, lb1, DIL_DILATIONS[1]),
                                        (o2_ref, l2_ref, ob2, lb2, DIL_DILATIONS[2])):
        rows = o_ref.shape[1]
        for r in range(dil):
            lb[pl.ds(r, rows, stride=dil), :] = l_ref[r]
            for h in range(DIL_HEADS):
                hs = slice(h * DIL_HEAD_DIM, (h + 1) * DIL_HEAD_DIM)
                ob[h, pl.ds(r, rows, stride=dil), :] = o_ref[r, :, hs].astype(F32)

    l0, l1, l2 = l0_ref[...], lb1[...], lb2[...]
    m = jnp.maximum(jnp.maximum(l0, l1), l2)
    e0, e1, e2 = jnp.exp(l0 - m), jnp.exp(l1 - m), jnp.exp(l2 - m)
    inv = 1.0 / (e0 + e1 + e2)
    w0, w1, w2 = e0 * inv, e1 * inv, e2 * inv
    rows = l0.shape[0]
    for h in range(DIL_HEADS):
        hs = slice(h * DIL_HEAD_DIM, (h + 1) * DIL_HEAD_DIM)

        def lane(w):
            return jnp.broadcast_to(w[:, h:h + 1], (rows, DIL_HEAD_DIM))

        o = lane(w0) * o0_ref[:, hs].astype(F32) + lane(w1) * ob1[h] + lane(w2) * ob2[h]
        y_ref[:, hs] = (_silu(g_ref[:, hs].astype(F32)) * o).astype(y_ref.dtype)


def _dil_combine(outs, lses, proj2d):
    m = proj2d.shape[0]
    d1, d2 = DIL_DILATIONS[1], DIL_DILATIONS[2]
    tm = CHUNK * d1
    sub = tm // d2
    per = outs[2].shape[2] // sub
    o_spec = pl.BlockSpec((tm, DIL_WIDTH), lambda i: (i, 0))
    l_spec = pl.BlockSpec((tm, LANES), lambda i: (i, 0))

    def grouped(width):
        return (pl.BlockSpec((None, d1, CHUNK, width), lambda i: (i, 0, 0, 0)),
                pl.BlockSpec((None, d2, sub, width), lambda i: (i // per, 0, i % per, 0)))

    o1_spec, o2_spec = grouped(DIL_WIDTH)
    l1_spec, l2_spec = grouped(LANES)
    return pl.pallas_call(
        _dil_combine_kernel,
        grid=(m // tm,),
        in_specs=[o_spec, o1_spec, o2_spec, l_spec, l1_spec, l2_spec,
                  pl.BlockSpec((tm, DIL_WIDTH), lambda i: (i, T_DG))],
        out_specs=o_spec,
        out_shape=jax.ShapeDtypeStruct((m, DIL_WIDTH), BF16),
        scratch_shapes=[pltpu.VMEM((DIL_HEADS, tm, DIL_HEAD_DIM), F32), pltpu.VMEM((DIL_HEADS, tm, DIL_HEAD_DIM), F32),
                        pltpu.VMEM((tm, LANES), F32), pltpu.VMEM((tm, LANES), F32)],
        compiler_params=_cparams(("parallel",)),
        name="dilated_combine",
    )(outs[0].reshape(m, DIL_WIDTH), outs[1], outs[2], lses[0].reshape(m, LANES), lses[1], lses[2], proj2d)


def _merge_kernel(ya_ref, yb_ref, yc_ref, wa_ref, wb_ref, wc_ref, ga_ref, gb_ref, gc_ref, o_ref):
    acc = _sigmoid(ga_ref[...].astype(F32)) * _dot(ya_ref[...], wa_ref[...])
    acc = acc + _sigmoid(gb_ref[...].astype(F32)) * _dot(yb_ref[...], wb_ref[...])
    acc = acc + _sigmoid(gc_ref[...].astype(F32)) * _dot(yc_ref[...], wc_ref[...])
    o_ref[...] = acc.astype(o_ref.dtype)


def _merge(ya, yb, yc, wa, wb, wc, proj2d, tm=512, tn=PROJ_TILE):
    m = ya.shape[0]
    d = wa.shape[1]

    def act(width):
        return pl.BlockSpec((tm, width), lambda j, i: (i, 0))

    def wgt(width):
        return pl.BlockSpec((width, tn), lambda j, i: (0, j))

    def gate(branch):
        return pl.BlockSpec((tm, tn), lambda j, i: (i, T_MG + branch * (d // tn) + j))

    return pl.pallas_call(
        _merge_kernel,
        grid=(d // tn, m // tm),
        in_specs=[act(SSD_INNER), act(RET_V), act(DIL_WIDTH), wgt(SSD_INNER), wgt(RET_V), wgt(DIL_WIDTH),
                  gate(0), gate(1), gate(2)],
        out_specs=pl.BlockSpec((tm, tn), lambda j, i: (i, j)),
        out_shape=jax.ShapeDtypeStruct((m, d), BF16),
        compiler_params=_cparams(("parallel", "parallel")),
        name="gated_merge",
    )(ya, yb, yc, wa, wb, wc, proj2d, proj2d, proj2d)


def _out_kernel(a_ref, w_ref, x_ref, g_ref, xo_ref, ho_ref):
    x = x_ref[...] + _dot(a_ref[...], w_ref[...])
    xo_ref[...] = x
    r = lax.rsqrt(jnp.mean(x * x, axis=-1, keepdims=True) + NORM_EPS)
    ho_ref[...] = (x * r * g_ref[...]).astype(ho_ref.dtype)


def _out_proj(merged, w_out, x2d, g_next, norm_dtype, tm=512):
    m, d = x2d.shape
    row = pl.BlockSpec((tm, d), lambda i: (i, 0))
    return pl.pallas_call(
        _out_kernel,
        grid=(m // tm,),
        in_specs=[row, pl.BlockSpec((d, d), lambda i: (0, 0)), row, pl.BlockSpec((1, d), lambda i: (0, 0))],
        out_specs=[row, row],
        out_shape=[jax.ShapeDtypeStruct((m, d), F32), jax.ShapeDtypeStruct((m, d), norm_dtype)],
        compiler_params=_cparams(("parallel",)),
        name="out_proj_residual_norm",
    )(merged, w_out, x2d, g_next.reshape(1, d))


def kernel(x, norm_g, w_in, conv_w, conv_b, dt_bias, a_log, d_skip, ssd_norm_g,
           w_o_ssd, w_o_ret, w_o_dil, w_out, final_norm_g):
    b, s, d = x.shape
    depth = w_in.shape[0]
    m = b * s
    assert s % (max(DIL_DILATIONS) * CHUNK * DIL_BLOCKS) == 0 and d == 2048
    cosf, sinf = _rope_tables(s)
    w_t = jnp.swapaxes(w_in, 1, 2)

    x2d = x.reshape(m, d)
    h = _rms_norm(x2d, norm_g[0], BF16)
    for layer in range(depth):
        proj2d = _projection(h, w_t, layer, cosf, sinf, n_tiles=N_MAIN_TILES, src_col=_main_src_col,
                             rope_tiles=(T_RQ, T_RK, T_DQ, T_DK), scaled_tiles=(T_RK, T_DQ), dil=1,
                             name="in_proj")
        dt_raw = _dt_projection(h, w_t, layer)
        proj = proj2d.reshape(b, s, N_MAIN)

        y_a = _ssd_branch(proj, dt_raw.reshape(b, s, LANES), conv_w[layer], conv_b[layer], dt_bias[layer],
                          a_log[layer], d_skip[layer], ssd_norm_g[layer])
        y_b = _ret_branch(proj)

        outs, lses = [], []
        for gi, dil in enumerate(DIL_DILATIONS):
            if dil == 1:
                qkv, tiles_qkv = proj2d.reshape(m // CHUNK, 1, CHUNK, N_MAIN), (T_DQ, T_DK, T_DV)
            else:
                first = W_DQ1_COL if gi == 1 else W_DQ2_COL
                qkv = _projection(h, w_t, layer, cosf, sinf, n_tiles=3,
                                  src_col=lambda j, f=first: (f // 8 + j * (PROJ_TILE // 8)) * 8,
                                  rope_tiles=(0, 1), scaled_tiles=(0,), dil=dil, name=f"dqkv_proj_d{dil}")
                tiles_qkv = (0, 1, 2)
            o, lse = _dil_group(qkv, tiles_qkv, dil, b)
            outs.append(o)
            lses.append(lse)
        y_c = _dil_combine(outs, lses, proj2d)

        merged = _merge(y_a.reshape(m, SSD_INNER), y_b.reshape(m, RET_V), y_c,
                        w_o_ssd[layer].astype(BF16), w_o_ret[layer].astype(BF16), w_o_dil[layer].astype(BF16),
                        proj2d)
        last = layer == depth - 1
        g_next = final_norm_g if last else norm_g[layer + 1]
        x2d, h = _out_proj(merged, w_out[layer].astype(BF16), x2d, g_next, F32 if last else BF16)
    return h.reshape(b, s, d)
```

```python
import functools
import math

import numpy as np
import jax
import jax.numpy as jnp
from jax import lax
from jax.experimental import pallas as pl
from jax.experimental.pallas import tpu as pltpu

F32 = jnp.float32
BF16 = jnp.bfloat16

NORM_EPS = 1e-6
CHUNK = 128
STEP_CHUNKS = 2
ROPE_THETA = 10000.0
LOG2E = 1.4426950408889634

SSD_INNER = 2048
SSD_HEAD_DIM = 64
SSD_HEADS = 32
SSD_GROUPS = 4
SSD_REP = SSD_HEADS // SSD_GROUPS
SSD_STATE = 128
SSD_CONV = 4
SSD_BC = SSD_GROUPS * SSD_STATE

RET_HEADS = 8
RET_QK_DIM = 128
RET_V_DIM = 256
RET_QK = RET_HEADS * RET_QK_DIM
RET_V = RET_HEADS * RET_V_DIM

DIL_DILATIONS = (1, 4, 16)
DIL_HEADS = 8
DIL_HEAD_DIM = 128
DIL_WIDTH = DIL_HEADS * DIL_HEAD_DIM
DIL_HEAD_GROUP = 4
DIL_MAX_BLOCKS = 4
QK_SCALE = DIL_HEAD_DIM ** -0.5

LANES = 128
MXU_COLS = 256
VMEM_LIMIT = 56 * 1024 * 1024
PROJ_TILE = 1024

W_DT_COL = 5120
W_AFTER_DT = W_DT_COL + SSD_HEADS
W_DQ1_COL = W_AFTER_DT + 9 * PROJ_TILE
W_DQ2_COL = W_AFTER_DT + 12 * PROJ_TILE

T_Z, T_XS, T_BC, T_RQ, T_RK, T_RV, T_RG, T_DQ, T_DK, T_DV, T_DG, T_MG = 0, 2, 4, 5, 6, 7, 9, 11, 12, 13, 14, 15
N_MAIN_TILES = 21
N_MAIN = N_MAIN_TILES * PROJ_TILE


def _main_src_col(j):
    sub = 8
    return (j * (PROJ_TILE // sub) + jnp.where(j < T_RQ, 0, SSD_HEADS // sub)
            + jnp.where(j < T_DG, 0, 6 * PROJ_TILE // sub)) * sub


def _cparams(sem):
    return pltpu.CompilerParams(dimension_semantics=sem, vmem_limit_bytes=VMEM_LIMIT)


def _sigmoid(v):
    return 1.0 / (1.0 + jnp.exp(-v))


def _silu(v):
    half = 0.5 * v
    return half + half * jnp.tanh(half)


def _dot(a, b):
    return jnp.dot(a, b, preferred_element_type=F32)


def _dot_nt(a, b):
    return lax.dot_general(a, b, (((1,), (1,)), ((), ())), preferred_element_type=F32)


def _norm_kernel(x_ref, g_ref, o_ref):
    x = x_ref[...]
    r = lax.rsqrt(jnp.mean(x * x, axis=-1, keepdims=True) + NORM_EPS)
    o_ref[...] = (x * r * g_ref[...]).astype(o_ref.dtype)


def _rms_norm(x2d, g, out_dtype, tm=512):
    m, d = x2d.shape
    return pl.pallas_call(
        _norm_kernel,
        grid=(m // tm,),
        in_specs=[pl.BlockSpec((tm, d), lambda i: (i, 0)),
                  pl.BlockSpec((1, d), lambda i: (0, 0))],
        out_specs=pl.BlockSpec((tm, d), lambda i: (i, 0)),
        out_shape=jax.ShapeDtypeStruct((m, d), out_dtype),
        compiler_params=_cparams(("parallel",)),
        name="rms_norm",
    )(x2d, g.reshape(1, d))


def _dt_kernel(a_ref, wt_ref, o_ref):
    res = _dot_nt(a_ref[...], wt_ref[...].astype(BF16))
    lane = lax.broadcasted_iota(jnp.int32, res.shape, 1)
    o_ref[...] = jnp.where(lane < SSD_HEADS, res, 0.0)


def _dt_projection(a, w_t, layer, tm=PROJ_TILE):
    m, k = a.shape
    return pl.pallas_call(
        _dt_kernel,
        grid=(m // tm,),
        in_specs=[pl.BlockSpec((tm, k), lambda i: (i, 0)),
                  pl.BlockSpec((None, LANES, k), lambda i: (layer, W_DT_COL // LANES, 0))],
        out_specs=pl.BlockSpec((tm, LANES), lambda i: (i, 0)),
        out_shape=jax.ShapeDtypeStruct((m, LANES), F32),
        compiler_params=_cparams(("parallel",)),
        name="dt_proj",
    )(a, w_t)


def _proj_kernel(a_ref, wt_ref, cos_ref, sin_ref, o_ref, wbuf, *scratch, rope_tiles, scaled_tiles, dil):
    j = pl.program_id(0)

    @pl.when(pl.program_id(1) == 0)
    def _():
        for c0 in range(0, wbuf.shape[0], LANES):
            wbuf[c0:c0 + LANES, :] = wt_ref[c0:c0 + LANES, :].astype(BF16)

    def any_of(tiles):
        hit = j == tiles[0]
        for t in tiles[1:]:
            hit = jnp.logical_or(hit, j == t)
        return hit

    def body(rope):
        if rope:
            factor = jnp.where(any_of(scaled_tiles), QK_SCALE, 1.0)
            cosf = cos_ref[...] * factor
            sinf = sin_ref[...] * factor
        res = _dot_nt(a_ref[...], wbuf[...])
        for c in range(res.shape[1] // LANES):
            cs = slice(c * LANES, (c + 1) * LANES)
            blk = res[:, cs]
            if rope:
                blk = blk * cosf + pltpu.roll(blk, DIL_HEAD_DIM // 2, 1) * sinf
            if dil == 1:
                o_ref[:, cs] = blk.astype(o_ref.dtype)
            else:
                acc = scratch[0]
                acc[c] = blk
                for r in range(dil):
                    o_ref[r, :, cs] = acc[c, pl.ds(r, o_ref.shape[1], stride=dil), :].astype(o_ref.dtype)

    is_rope = any_of(rope_tiles)
    pl.when(is_rope)(functools.partial(body, True))
    pl.when(jnp.logical_not(is_rope))(functools.partial(body, False))


def _projection(a, w_t, layer, cosf, sinf, *, n_tiles, src_col, rope_tiles, scaled_tiles, dil, name):
    m, k = a.shape
    s = cosf.shape[0]
    tn = PROJ_TILE
    n = n_tiles * tn
    tm = PROJ_TILE
    tab = pl.BlockSpec((tm, DIL_HEAD_DIM), lambda j, i: (i % (s // tm), 0))
    if dil == 1:
        out_spec = pl.BlockSpec((tm, tn), lambda j, i: (i, j))
        out_shape = jax.ShapeDtypeStruct((m, n), BF16)
        scratch = []
    else:
        out_spec = pl.BlockSpec((None, dil, tm // dil, tn), lambda j, i: (i, 0, 0, j))
        out_shape = jax.ShapeDtypeStruct((m // tm, dil, tm // dil, n), BF16)
        scratch = [pltpu.VMEM((tn // LANES, tm, LANES), F32)]
    return pl.pallas_call(
        functools.partial(_proj_kernel, rope_tiles=rope_tiles, scaled_tiles=scaled_tiles, dil=dil),
        grid=(n_tiles, m // tm),
        in_specs=[pl.BlockSpec((tm, k), lambda j, i: (i, 0)),
                  pl.BlockSpec((None, pl.Element(tn), pl.Element(k)), lambda j, i: (layer, src_col(j), 0)),
                  tab, tab],
        out_specs=out_spec,
        out_shape=out_shape,
        scratch_shapes=[pltpu.VMEM((tn, k), BF16)] + scratch,
        compiler_params=_cparams(("parallel", "arbitrary")),
        name=name,
    )(a, w_t, cosf, sinf)


def _ssd_kernel(z_ref, xs_ref, bc_ref, dt_ref, cwx_ref, cwbc_ref, cbx_ref, cbbc_ref,
                dtb_ref, alog_ref, dskip_ref, ng_ref, o_ref, xbuf, bcbuf, state, ybuf):
    c = pl.program_id(1)

    n_sub = xs_ref.shape[0] // CHUNK
    hist = slice(0, CHUNK)

    @pl.when(c == 0)
    def _():
        xbuf[hist, :] = jnp.zeros((CHUNK, SSD_INNER), BF16)
        bcbuf[hist, :] = jnp.zeros((CHUNK, 2 * SSD_BC), BF16)
        state[...] = jnp.zeros_like(state)

    xbuf[CHUNK:, :] = xs_ref[...]
    bcbuf[CHUNK:, :] = bc_ref[...]

    srow = lax.broadcasted_iota(jnp.int32, (CHUNK, 2 * CHUNK), 0)
    scol = lax.broadcasted_iota(jnp.int32, (CHUNK, 2 * CHUNK), 1)
    shifts = [jnp.where(scol == CHUNK + srow - back, 1.0, 0.0).astype(BF16) for back in range(1, SSD_CONV)]
    row = lax.broadcasted_iota(jnp.int32, (CHUNK, CHUNK), 0)
    col = lax.broadcasted_iota(jnp.int32, (CHUNK, CHUNK), 1)
    causal = row >= col
    tril = jnp.where(causal, 1.0, 0.0).astype(F32)
    lane_lo = lax.broadcasted_iota(jnp.int32, (CHUNK, LANES), 1) < SSD_HEAD_DIM
    a_neg = -jnp.exp(alog_ref[...])

    for sub in range(n_sub):
        rows = slice(sub * CHUNK, (sub + 1) * CHUNK)

        def conv_silu(buf, w_ref, b_ref):
            ext = buf[sub * CHUNK:(sub + 2) * CHUNK, :]
            acc = b_ref[...] + w_ref[SSD_CONV - 1:SSD_CONV, :] * ext[CHUNK:, :].astype(F32)
            for back in range(1, SSD_CONV):
                tap = SSD_CONV - 1 - back
                acc = acc + w_ref[tap:tap + 1, :] * _dot(shifts[back - 1], ext)
            return _silu(acc)

        xs = conv_silu(xbuf, cwx_ref, cbx_ref)
        bc = conv_silu(bcbuf, cwbc_ref, cbbc_ref)
        xs_b = xs.astype(BF16)

        v = dt_ref[rows, :] + dtb_ref[...]
        dt = jnp.maximum(v, 0.0) + jnp.log1p(jnp.exp(-jnp.abs(v)))
        da = dt * a_neg
        acs = jnp.dot(tril, da, preferred_element_type=F32, precision=lax.Precision.HIGHEST)
        acs2 = acs * LOG2E
        ldt = jnp.log2(dt)
        arow_t = (acs2 - ldt).T
        total2 = acs2[CHUNK - 1:CHUNK, :]
        w_t = jnp.exp2(total2 - acs2 + ldt).T

        for g in range(SSD_GROUPS):
            bm_g = bc[:, g * SSD_STATE:(g + 1) * SSD_STATE]
            cm_g = bc[:, SSD_BC + g * SSD_STATE:SSD_BC + (g + 1) * SSD_STATE]
            cb = _dot_nt(cm_g.astype(BF16), bm_g.astype(BF16))
            bm_gt = bm_g.T
            for pair in range(SSD_REP // 2):
                h0 = g * SSD_REP + 2 * pair
                ps = slice(h0 * SSD_HEAD_DIM, (h0 + 2) * SSD_HEAD_DIM)
                xs_p = xs_b[:, ps]
                st_old = state[h0 // 2]
                rhs = jnp.concatenate([xs_p, st_old.astype(BF16)], axis=0)
                ys, sts, cds = [], [], []
                for h in (h0, h0 + 1):
                    a_col = jnp.broadcast_to(acs2[:, h:h + 1], (CHUNK, CHUNK))
                    decay = jnp.exp2(jnp.where(causal, a_col - arow_t[h:h + 1, :], -jnp.inf))
                    e_col = jnp.exp2(a_col)
                    lhs = jnp.concatenate([(cb * decay).astype(BF16), (cm_g * e_col).astype(BF16)], axis=1)
                    ys.append(_dot(lhs, rhs))
                    sts.append(_dot((bm_gt * w_t[h:h + 1, :]).astype(BF16), xs_p))
                    cds.append(e_col[CHUNK - 1:CHUNK, :])
                ybuf[rows, ps] = jnp.where(lane_lo, ys[0], ys[1])
                chunk_decay = jnp.where(lane_lo[0:1, :], cds[0], cds[1])
                state[h0 // 2] = st_old * chunk_decay + jnp.where(lane_lo, sts[0], sts[1])

        y = (ybuf[rows, :] + xs * dskip_ref[...]) * _silu(z_ref[rows, :].astype(F32))
        r = lax.rsqrt(jnp.mean(y * y, axis=-1, keepdims=True) + NORM_EPS)
        o_ref[rows, :] = (y * r * ng_ref[...]).astype(o_ref.dtype)

    xbuf[hist, :] = xbuf[n_sub * CHUNK:, :]
    bcbuf[hist, :] = bcbuf[n_sub * CHUNK:, :]


def _ssd_branch(proj, dt_raw, conv_w, conv_b, dt_bias, a_log, d_skip, norm_g):
    b, s, _ = proj.shape
    rows = STEP_CHUNKS * CHUNK
    nc = s // rows

    def col(width, tile):
        return pl.BlockSpec((None, rows, width), lambda bi, ci: (bi, ci, tile * PROJ_TILE // width))

    def const(shape):
        return pl.BlockSpec(shape, lambda bi, ci: (0,) * len(shape))

    pad = LANES - SSD_HEADS
    dtb = jnp.pad(dt_bias, (0, pad))[None, :]
    alog = jnp.pad(a_log, (0, pad))[None, :]
    dskip = jnp.repeat(d_skip, SSD_HEAD_DIM)[None, :]
    return pl.pallas_call(
        _ssd_kernel,
        grid=(b, nc),
        in_specs=[col(SSD_INNER, T_Z), col(SSD_INNER, T_XS), col(2 * SSD_BC, T_BC),
                  pl.BlockSpec((None, rows, LANES), lambda bi, ci: (bi, ci, 0)),
                  const((SSD_CONV, SSD_INNER)), const((SSD_CONV, 2 * SSD_BC)),
                  const((1, SSD_INNER)), const((1, 2 * SSD_BC)),
                  const((1, LANES)), const((1, LANES)), const((1, SSD_INNER)), const((1, SSD_INNER))],
        out_specs=pl.BlockSpec((None, rows, SSD_INNER), lambda bi, ci: (bi, ci, 0)),
        out_shape=jax.ShapeDtypeStruct((b, s, SSD_INNER), BF16),
        scratch_shapes=[pltpu.VMEM((rows + CHUNK, SSD_INNER), BF16),
                        pltpu.VMEM((rows + CHUNK, 2 * SSD_BC), BF16),
                        pltpu.VMEM((SSD_HEADS // 2, SSD_STATE, 2 * SSD_HEAD_DIM), F32),
                        pltpu.VMEM((rows, SSD_INNER), F32)],
        compiler_params=_cparams(("parallel", "arbitrary")),
        name="ssd_mixer",
    )(proj, proj, proj, dt_raw, conv_w[:, :SSD_INNER], conv_w[:, SSD_INNER:],
      conv_b[None, :SSD_INNER], conv_b[None, SSD_INNER:], dtb, alog, dskip, norm_g[None, :])


def _rope_tables(s):
    inv_freq = ROPE_THETA ** (-jnp.arange(0, RET_QK_DIM, 2, dtype=F32) / RET_QK_DIM)
    ang = jnp.arange(s, dtype=F32)[:, None] * inv_freq[None, :]
    cos, sin = jnp.cos(ang), jnp.sin(ang)
    return jnp.concatenate([cos, cos], axis=-1), jnp.concatenate([-sin, sin], axis=-1)


_LOG_GAMMA = [float(np.log(np.float32(1.0) - np.exp2(np.float32(-5.0 - h)))) for h in range(RET_HEADS)]


def _ret_kernel(q_ref, k_ref, v_lo_ref, v_hi_ref, g_lo_ref, g_hi_ref, o_ref, state, dmask, qdec, kdec):
    c = pl.program_id(1)

    @pl.when(c == 0)
    def _():
        state[...] = jnp.zeros_like(state)
        li = lax.broadcasted_iota(jnp.int32, (CHUNK, CHUNK), 0).astype(F32)
        si = lax.broadcasted_iota(jnp.int32, (CHUNK, CHUNK), 1).astype(F32)
        rel = li - si
        for h in range(RET_HEADS):
            lg = _LOG_GAMMA[h]
            dmask[h] = jnp.where(rel >= 0, jnp.exp(jnp.maximum(rel, 0.0) * lg), 0.0)
            qdec[h] = jnp.exp((li + 1.0) * lg)
            kdec[h] = jnp.exp((CHUNK - 1.0 - li) * lg)

    for sub in range(q_ref.shape[0] // CHUNK):
        rows = slice(sub * CHUNK, (sub + 1) * CHUNK)
        for h in range(RET_HEADS):
            qs = slice(h * RET_QK_DIM, (h + 1) * RET_QK_DIM)
            vs = slice(h * RET_V_DIM, (h + 1) * RET_V_DIM)
            half = RET_HEADS // 2
            v_ref, g_ref = (v_lo_ref, g_lo_ref) if h < half else (v_hi_ref, g_hi_ref)
            hv = slice((h % half) * RET_V_DIM, (h % half + 1) * RET_V_DIM)
            qr = q_ref[rows, qs]
            kr = k_ref[rows, qs]
            v_h = v_ref[rows, hv]
            scores = _dot_nt(qr, kr) * dmask[h]
            inner = _dot(scores.astype(BF16), v_h)
            st_old = state[h]
            cross = _dot((qr.astype(F32) * qdec[h]).astype(BF16), st_old.astype(BF16))
            kv = _dot((kr.astype(F32) * kdec[h]).T.astype(BF16), v_h)
            state[h] = st_old * math.exp(CHUNK * _LOG_GAMMA[h]) + kv
            o = inner + cross
            o = o * lax.rsqrt(jnp.mean(o * o, axis=-1, keepdims=True) + NORM_EPS)
            o_ref[rows, vs] = (_silu(g_ref[rows, hv].astype(F32)) * o).astype(o_ref.dtype)


def _ret_branch(proj):
    b, s, _ = proj.shape
    rows = STEP_CHUNKS * CHUNK
    nc = s // rows

    def col(tile):
        return pl.BlockSpec((None, rows, PROJ_TILE), lambda bi, ci: (bi, ci, tile))

    return pl.pallas_call(
        _ret_kernel,
        grid=(b, nc),
        in_specs=[col(T_RQ), col(T_RK), col(T_RV), col(T_RV + 1), col(T_RG), col(T_RG + 1)],
        out_specs=pl.BlockSpec((None, rows, RET_V), lambda bi, ci: (bi, ci, 0)),
        out_shape=jax.ShapeDtypeStruct((b, s, RET_V), BF16),
        scratch_shapes=[pltpu.VMEM((RET_HEADS, RET_QK_DIM, RET_V_DIM), F32),
                        pltpu.VMEM((RET_HEADS, CHUNK, CHUNK), F32),
                        pltpu.VMEM((RET_HEADS, CHUNK, CHUNK), F32),
                        pltpu.VMEM((RET_HEADS, CHUNK, CHUNK), F32)],
        compiler_params=_cparams(("parallel", "arbitrary")),
        name="retention_mixer",
    )(proj, proj, proj, proj, proj, proj)


def _dil_kernel(q_ref, k_ref, v_ref, o_ref, lse_ref, kbuf, vbuf, *, blocks):
    n = pl.program_id(2)
    ext = 2 * DIL_HEAD_DIM
    first = slice(0, CHUNK)
    last = slice(blocks * CHUNK, (blocks + 1) * CHUNK)

    @pl.when(n == 0)
    def _():
        kbuf[first, :] = jnp.zeros((CHUNK, DIL_WIDTH), BF16)
        for h in range(DIL_HEADS):
            vbuf[first, h * ext:h * ext + DIL_HEAD_DIM] = jnp.zeros((CHUNK, DIL_HEAD_DIM), BF16)
            vbuf[:, h * ext + DIL_HEAD_DIM:(h + 1) * ext] = jnp.ones(((blocks + 1) * CHUNK, DIL_HEAD_DIM), BF16)

    parts, part_rows = q_ref.shape[0], q_ref.shape[1]
    piece = min(part_rows, CHUNK)

    def pieces(blk):
        return [((blk * CHUNK + off) // part_rows, (blk * CHUNK + off) % part_rows, off)
                for off in range(0, CHUNK, piece)]

    for pi in range(parts):
        dst = slice(CHUNK + pi * part_rows, CHUNK + (pi + 1) * part_rows)
        kbuf[dst, :] = k_ref[pi]
        for h in range(DIL_HEADS):
            vbuf[dst, h * ext:h * ext + DIL_HEAD_DIM] = v_ref[pi, :, h * DIL_HEAD_DIM:(h + 1) * DIL_HEAD_DIM]

    qi = lax.broadcasted_iota(jnp.int32, (CHUNK, 2 * CHUNK), 0)
    kj = lax.broadcasted_iota(jnp.int32, (CHUNK, 2 * CHUNK), 1)
    kl = kj & (CHUNK - 1)
    band = jnp.logical_or(jnp.logical_and(kj >= CHUNK, kl <= qi), jnp.logical_and(kj < CHUNK, kl >= qi))
    band_first = jnp.logical_and(band, jnp.logical_or(kj >= CHUNK, n > 0))
    lane = lax.broadcasted_iota(jnp.int32, (CHUNK, LANES), 1)

    for blk in range(blocks):
        keys = slice(blk * CHUNK, (blk + 2) * CHUNK)
        mask = band_first if blk == 0 else band
        block_pieces = pieces(blk)

        def q_head(cols):
            return jnp.concatenate([q_ref[pi, r0:r0 + piece, cols] for pi, r0, _ in block_pieces], axis=0)

        lse_all = jnp.zeros((CHUNK, LANES), F32)
        for h0 in range(0, DIL_HEADS, DIL_HEAD_GROUP):
            heads = range(h0, h0 + DIL_HEAD_GROUP)
            sl = {h: slice(h * DIL_HEAD_DIM, (h + 1) * DIL_HEAD_DIM) for h in heads}
            sc = {h: jnp.where(mask, _dot_nt(q_head(sl[h]), kbuf[keys, sl[h]]), -jnp.inf) for h in heads}
            mx = {h: jnp.max(sc[h], axis=-1, keepdims=True) for h in heads}
            pr = {h: jnp.exp(sc[h] - mx[h]).astype(BF16) for h in heads}
            res = {h: _dot(pr[h], vbuf[keys, h * ext:(h + 1) * ext]) for h in heads}
            for h in heads:
                den = res[h][:, DIL_HEAD_DIM:]
                o_h = (res[h][:, :DIL_HEAD_DIM] / den).astype(o_ref.dtype)
                for pi, r0, off in block_pieces:
                    o_ref[pi, r0:r0 + piece, sl[h]] = o_h[off:off + piece]
                lse_all = jnp.where(lane == h, mx[h] + jnp.log(den), lse_all)
        for pi, r0, off in block_pieces:
            lse_ref[pi, r0:r0 + piece, :] = lse_all[off:off + piece]

    kbuf[first, :] = kbuf[last, :]
    for h in range(DIL_HEADS):
        vbuf[first, h * ext:h * ext + DIL_HEAD_DIM] = vbuf[last, h * ext:h * ext + DIL_HEAD_DIM]


def _dil_group(qkv, tiles_qkv, dil, batch):
    tiles, _, rows, _ = qkv.shape
    seq_blocks = tiles * rows // batch // CHUNK
    blocks = min(DIL_MAX_BLOCKS, seq_blocks)
    assert rows <= blocks * CHUNK and seq_blocks % blocks == 0
    parts = blocks * CHUNK // rows
    nb = seq_blocks // blocks

    def part(width, tile):
        return pl.BlockSpec((parts, None, rows, width), lambda bi, r, n: (bi * nb + n, r, 0, tile))

    return pl.pallas_call(
        functools.partial(_dil_kernel, blocks=blocks),
        grid=(batch, dil, nb),
        in_specs=[part(DIL_WIDTH, t) for t in tiles_qkv],
        out_specs=[part(DIL_WIDTH, 0), part(LANES, 0)],
        out_shape=[jax.ShapeDtypeStruct((tiles, dil, rows, DIL_WIDTH), BF16),
                   jax.ShapeDtypeStruct((tiles, dil, rows, LANES), F32)],
        scratch_shapes=[pltpu.VMEM(((blocks + 1) * CHUNK, DIL_WIDTH), BF16),
                        pltpu.VMEM(((blocks + 1) * CHUNK, 2 * DIL_WIDTH), BF16)],
        compiler_params=_cparams(("parallel", "parallel", "arbitrary")),
        name=f"dilated_attention_d{dil}",
    )(qkv, qkv, qkv)


def _dil_combine_kernel(o0_ref, o1_ref, o2_ref, l0_ref, l1_ref, l2_ref, g_ref, y_ref, ob1, ob2, lb1, lb2):
    for (o_ref, l_ref, ob, lb, dil) in ((o1_ref, l1_ref, ob1, lb1, DIL_DILATIONS[1]),
                                        (o2_ref, l2_ref, ob2, lb2, DIL_DILATIONS[2])):
        rows = o_ref.shape[1]
        for r in range(dil):
            lb[pl.ds(r, rows, stride=dil), :] = l_ref[r]
            for h in range(DIL_HEADS):
                hs = slice(h * DIL_HEAD_DIM, (h + 1) * DIL_HEAD_DIM)
                ob[h, pl.ds(r, rows, stride=dil), :] = o_ref[r, :, hs].astype(F32)

    l0, l1, l2 = l0_ref[...], lb1[...], lb2[...]
    m = jnp.maximum(jnp.maximum(l0, l1), l2)
    e0, e1, e2 = jnp.exp(l0 - m), jnp.exp(l1 - m), jnp.exp(l2 - m)
    inv = 1.0 / (e0 + e1 + e2)
    w0, w1, w2 = e0 * inv, e1 * inv, e2 * inv
    rows = l0.shape[0]
    for h in range(DIL_HEADS):
        hs = slice(h * DIL_HEAD_DIM, (h + 1) * DIL_HEAD_DIM)

        def lane(w):
            return jnp.broadcast_to(w[:, h:h + 1], (rows, DIL_HEAD_DIM))

        o = lane(w0) * o0_ref[:, hs].astype(F32) + lane(w1) * ob1[h] + lane(w2) * ob2[h]
        y_ref[:, hs] = (_silu(g_ref[:, hs].astype(F32)) * o).astype(y_ref.dtype)


def _dil_combine(outs, lses, proj2d):
    m = proj2d.shape[0]
    d1, d2 = DIL_DILATIONS[1], DIL_DILATIONS[2]
    tm = CHUNK * d1
    o_spec = pl.BlockSpec((tm, DIL_WIDTH), lambda i: (i, 0))
    l_spec = pl.BlockSpec((tm, LANES), lambda i: (i, 0))

    def grouped_spec(arr, dil, width):
        sub = tm // dil
        per = arr.shape[2] // sub
        return pl.BlockSpec((None, dil, sub, width), lambda i: (i // per, 0, i % per, 0))

    def grouped(width):
        return grouped_spec(outs[1], d1, width), grouped_spec(outs[2], d2, width)

    o1_spec, o2_spec = grouped(DIL_WIDTH)
    l1_spec, l2_spec = grouped(LANES)
    return pl.pallas_call(
        _dil_combine_kernel,
        grid=(m // tm,),
        in_specs=[o_spec, o1_spec, o2_spec, l_spec, l1_spec, l2_spec,
                  pl.BlockSpec((tm, DIL_WIDTH), lambda i: (i, T_DG))],
        out_specs=o_spec,
        out_shape=jax.ShapeDtypeStruct((m, DIL_WIDTH), BF16),
        scratch_shapes=[pltpu.VMEM((DIL_HEADS, tm, DIL_HEAD_DIM), F32), pltpu.VMEM((DIL_HEADS, tm, DIL_HEAD_DIM), F32),
                        pltpu.VMEM((tm, LANES), F32), pltpu.VMEM((tm, LANES), F32)],
        compiler_params=_cparams(("parallel",)),
        name="dilated_combine",
    )(outs[0].reshape(m, DIL_WIDTH), outs[1], outs[2], lses[0].reshape(m, LANES), lses[1], lses[2], proj2d)


def _merge_kernel(ya_ref, yb_ref, yc_ref, wa_ref, wb_ref, wc_ref, ga_ref, gb_ref, gc_ref, o_ref):
    acc = _sigmoid(ga_ref[...].astype(F32)) * _dot(ya_ref[...], wa_ref[...])
    acc = acc + _sigmoid(gb_ref[...].astype(F32)) * _dot(yb_ref[...], wb_ref[...])
    acc = acc + _sigmoid(gc_ref[...].astype(F32)) * _dot(yc_ref[...], wc_ref[...])
    o_ref[...] = acc.astype(o_ref.dtype)


def _merge(ya, yb, yc, wa, wb, wc, proj2d, tm=512, tn=PROJ_TILE):
    m = ya.shape[0]
    d = wa.shape[1]

    def act(width):
        return pl.BlockSpec((tm, width), lambda j, i: (i, 0))

    def wgt(width):
        return pl.BlockSpec((width, tn), lambda j, i: (0, j))

    def gate(branch):
        return pl.BlockSpec((tm, tn), lambda j, i: (i, T_MG + branch * (d // tn) + j))

    return pl.pallas_call(
        _merge_kernel,
        grid=(d // tn, m // tm),
        in_specs=[act(SSD_INNER), act(RET_V), act(DIL_WIDTH), wgt(SSD_INNER), wgt(RET_V), wgt(DIL_WIDTH),
                  gate(0), gate(1), gate(2)],
        out_specs=pl.BlockSpec((tm, tn), lambda j, i: (i, j)),
        out_shape=jax.ShapeDtypeStruct((m, d), BF16),
        compiler_params=_cparams(("parallel", "parallel")),
        name="gated_merge",
    )(ya, yb, yc, wa, wb, wc, proj2d, proj2d, proj2d)


def _out_kernel(a_ref, w_ref, x_ref, g_ref, xo_ref, ho_ref):
    x = x_ref[...] + _dot(a_ref[...], w_ref[...])
    xo_ref[...] = x
    r = lax.rsqrt(jnp.mean(x * x, axis=-1, keepdims=True) + NORM_EPS)
    ho_ref[...] = (x * r * g_ref[...]).astype(ho_ref.dtype)


def _out_proj(merged, w_out, x2d, g_next, norm_dtype, tm=512):
    m, d = x2d.shape
    row = pl.BlockSpec((tm, d), lambda i: (i, 0))
    return pl.pallas_call(
        _out_kernel,
        grid=(m // tm,),
        in_specs=[row, pl.BlockSpec((d, d), lambda i: (0, 0)), row, pl.BlockSpec((1, d), lambda i: (0, 0))],
        out_specs=[row, row],
        out_shape=[jax.ShapeDtypeStruct((m, d), F32), jax.ShapeDtypeStruct((m, d), norm_dtype)],
        compiler_params=_cparams(("parallel",)),
        name="out_proj_residual_norm",
    )(merged, w_out, x2d, g_next.reshape(1, d))


def kernel(x, norm_g, w_in, conv_w, conv_b, dt_bias, a_log, d_skip, ssd_norm_g,
           w_o_ssd, w_o_ret, w_o_dil, w_out, final_norm_g):
    b, s, d = x.shape
    depth = w_in.shape[0]
    m = b * s
    assert s % (max(DIL_DILATIONS) * CHUNK) == 0 and s % PROJ_TILE == 0 and d == 2048
    cosf, sinf = _rope_tables(s)
    w_t = jnp.swapaxes(w_in, 1, 2)

    x2d = x.reshape(m, d)
    h = _rms_norm(x2d, norm_g[0], BF16)
    for layer in range(depth):
        proj2d = _projection(h, w_t, layer, cosf, sinf, n_tiles=N_MAIN_TILES, src_col=_main_src_col,
                             rope_tiles=(T_RQ, T_RK, T_DQ, T_DK), scaled_tiles=(T_RK, T_DQ), dil=1,
                             name="in_proj")
        dt_raw = _dt_projection(h, w_t, layer)
        proj = proj2d.reshape(b, s, N_MAIN)

        y_a = _ssd_branch(proj, dt_raw.reshape(b, s, LANES), conv_w[layer], conv_b[layer], dt_bias[layer],
                          a_log[layer], d_skip[layer], ssd_norm_g[layer])
        y_b = _ret_branch(proj)

        outs, lses = [], []
        for gi, dil in enumerate(DIL_DILATIONS):
            if dil == 1:
                qkv, tiles_qkv = proj2d.reshape(m // CHUNK, 1, CHUNK, N_MAIN), (T_DQ, T_DK, T_DV)
            else:
                first = W_DQ1_COL if gi == 1 else W_DQ2_COL
                qkv = _projection(h, w_t, layer, cosf, sinf, n_tiles=3,
                                  src_col=lambda j, f=first: (f // 8 + j * (PROJ_TILE // 8)) * 8,
                                  rope_tiles=(0, 1), scaled_tiles=(0,), dil=dil, name=f"dqkv_proj_d{dil}")
                tiles_qkv = (0, 1, 2)
            o, lse = _dil_group(qkv, tiles_qkv, dil, b)
            outs.append(o)
            lses.append(lse)
        y_c = _dil_combine(outs, lses, proj2d)

        merged = _merge(y_a.reshape(m, SSD_INNER), y_b.reshape(m, RET_V), y_c,
                        w_o_ssd[layer].astype(BF16), w_o_ret[layer].astype(BF16), w_o_dil[layer].astype(BF16),
                        proj2d)
        last = layer == depth - 1
        g_next = final_norm_g if last else norm_g[layer + 1]
        x2d, h = _out_proj(merged, w_out[layer].astype(BF16), x2d, g_next, F32 if last else BF16)
    return h.reshape(b, s, d)
```

```python
import functools
import math

import numpy as np
import jax
import jax.numpy as jnp
from jax import lax
from jax.experimental import pallas as pl
from jax.experimental.pallas import tpu as pltpu

F32 = jnp.float32
BF16 = jnp.bfloat16

NORM_EPS = 1e-6
CHUNK = 128
STEP_CHUNKS = 4
ROPE_THETA = 10000.0
LOG2E = 1.4426950408889634

SSD_INNER = 2048
SSD_HEAD_DIM = 64
SSD_HEADS = 32
SSD_GROUPS = 4
SSD_REP = SSD_HEADS // SSD_GROUPS
SSD_STATE = 128
SSD_CONV = 4
SSD_BC = SSD_GROUPS * SSD_STATE

RET_HEADS = 8
RET_QK_DIM = 128
RET_V_DIM = 256
RET_QK = RET_HEADS * RET_QK_DIM
RET_V = RET_HEADS * RET_V_DIM

DIL_DILATIONS = (1, 4, 16)
DIL_HEADS = 8
DIL_HEAD_DIM = 128
DIL_WIDTH = DIL_HEADS * DIL_HEAD_DIM
DIL_HEAD_GROUP = 4
DIL_MAX_BLOCKS = 8
QK_SCALE = DIL_HEAD_DIM ** -0.5

LANES = 128
MXU_COLS = 256
VMEM_LIMIT = 56 * 1024 * 1024
PROJ_TILE = 1024

W_DT_COL = 5120
W_AFTER_DT = W_DT_COL + SSD_HEADS
W_DQ1_COL = W_AFTER_DT + 9 * PROJ_TILE
W_DQ2_COL = W_AFTER_DT + 12 * PROJ_TILE

T_Z, T_XS, T_BC, T_RQ, T_RK, T_RV, T_RG, T_DQ, T_DK, T_DV, T_DG, T_MG = 0, 2, 4, 5, 6, 7, 9, 11, 12, 13, 14, 15
N_MAIN_TILES = 21
N_MAIN = N_MAIN_TILES * PROJ_TILE


def _main_src_col(j):
    sub = 8
    return (j * (PROJ_TILE // sub) + jnp.where(j < T_RQ, 0, SSD_HEADS // sub)
            + jnp.where(j < T_DG, 0, 6 * PROJ_TILE // sub)) * sub


def _cparams(sem):
    return pltpu.CompilerParams(dimension_semantics=sem, vmem_limit_bytes=VMEM_LIMIT)


def _sigmoid(v):
    return 1.0 / (1.0 + jnp.exp(-v))


def _silu(v):
    half = 0.5 * v
    return half + half * jnp.tanh(half)


def _dot(a, b):
    return jnp.dot(a, b, preferred_element_type=F32)


def _dot_nt(a, b):
    return lax.dot_general(a, b, (((1,), (1,)), ((), ())), preferred_element_type=F32)


def _norm_kernel(x_ref, g_ref, o_ref):
    x = x_ref[...]
    r = lax.rsqrt(jnp.mean(x * x, axis=-1, keepdims=True) + NORM_EPS)
    o_ref[...] = (x * r * g_ref[...]).astype(o_ref.dtype)


def _rms_norm(x2d, g, out_dtype, tm=512):
    m, d = x2d.shape
    return pl.pallas_call(
        _norm_kernel,
        grid=(m // tm,),
        in_specs=[pl.BlockSpec((tm, d), lambda i: (i, 0)),
                  pl.BlockSpec((1, d), lambda i: (0, 0))],
        out_specs=pl.BlockSpec((tm, d), lambda i: (i, 0)),
        out_shape=jax.ShapeDtypeStruct((m, d), out_dtype),
        compiler_params=_cparams(("parallel",)),
        name="rms_norm",
    )(x2d, g.reshape(1, d))


def _dt_kernel(a_ref, wt_ref, o_ref):
    res = _dot_nt(a_ref[...], wt_ref[...].astype(BF16))
    lane = lax.broadcasted_iota(jnp.int32, res.shape, 1)
    o_ref[...] = jnp.where(lane < SSD_HEADS, res, 0.0)


def _dt_projection(a, w_t, layer, tm=PROJ_TILE):
    m, k = a.shape
    return pl.pallas_call(
        _dt_kernel,
        grid=(m // tm,),
        in_specs=[pl.BlockSpec((tm, k), lambda i: (i, 0)),
                  pl.BlockSpec((None, LANES, k), lambda i: (layer, W_DT_COL // LANES, 0))],
        out_specs=pl.BlockSpec((tm, LANES), lambda i: (i, 0)),
        out_shape=jax.ShapeDtypeStruct((m, LANES), F32),
        compiler_params=_cparams(("parallel",)),
        name="dt_proj",
    )(a, w_t)


def _proj_kernel(a_ref, wt_ref, cos_ref, sin_ref, o_ref, wbuf, *scratch, rope_tiles, scaled_tiles, dil):
    j = pl.program_id(0)

    @pl.when(pl.program_id(1) == 0)
    def _():
        for c0 in range(0, wbuf.shape[0], LANES):
            wbuf[c0:c0 + LANES, :] = wt_ref[c0:c0 + LANES, :].astype(BF16)

    def any_of(tiles):
        hit = j == tiles[0]
        for t in tiles[1:]:
            hit = jnp.logical_or(hit, j == t)
        return hit

    def body(rope):
        if rope:
            factor = jnp.where(any_of(scaled_tiles), QK_SCALE, 1.0)
            cosf = cos_ref[...] * factor
            sinf = sin_ref[...] * factor
        res = _dot_nt(a_ref[...], wbuf[...])
        for c in range(res.shape[1] // LANES):
            cs = slice(c * LANES, (c + 1) * LANES)
            blk = res[:, cs]
            if rope:
                blk = blk * cosf + pltpu.roll(blk, DIL_HEAD_DIM // 2, 1) * sinf
            if dil == 1:
                o_ref[:, cs] = blk.astype(o_ref.dtype)
            else:
                acc = scratch[0]
                acc[c] = blk
                for r in range(dil):
                    o_ref[r, :, cs] = acc[c, pl.ds(r, o_ref.shape[1], stride=dil), :].astype(o_ref.dtype)

    is_rope = any_of(rope_tiles)
    pl.when(is_rope)(functools.partial(body, True))
    pl.when(jnp.logical_not(is_rope))(functools.partial(body, False))


def _projection(a, w_t, layer, cosf, sinf, *, n_tiles, src_col, rope_tiles, scaled_tiles, dil, name):
    m, k = a.shape
    s = cosf.shape[0]
    tn = PROJ_TILE
    n = n_tiles * tn
    tm = PROJ_TILE
    tab = pl.BlockSpec((tm, DIL_HEAD_DIM), lambda j, i: (i % (s // tm), 0))
    if dil == 1:
        out_spec = pl.BlockSpec((tm, tn), lambda j, i: (i, j))
        out_shape = jax.ShapeDtypeStruct((m, n), BF16)
        scratch = []
    else:
        out_spec = pl.BlockSpec((None, dil, tm // dil, tn), lambda j, i: (i, 0, 0, j))
        out_shape = jax.ShapeDtypeStruct((m // tm, dil, tm // dil, n), BF16)
        scratch = [pltpu.VMEM((tn // LANES, tm, LANES), F32)]
    return pl.pallas_call(
        functools.partial(_proj_kernel, rope_tiles=rope_tiles, scaled_tiles=scaled_tiles, dil=dil),
        grid=(n_tiles, m // tm),
        in_specs=[pl.BlockSpec((tm, k), lambda j, i: (i, 0)),
                  pl.BlockSpec((None, pl.Element(tn), pl.Element(k)), lambda j, i: (layer, src_col(j), 0)),
                  tab, tab],
        out_specs=out_spec,
        out_shape=out_shape,
        scratch_shapes=[pltpu.VMEM((tn, k), BF16)] + scratch,
        compiler_params=_cparams(("parallel", "arbitrary")),
        name=name,
    )(a, w_t, cosf, sinf)


def _ssd_kernel(z_ref, xs_ref, bc_ref, dt_ref, cwx_ref, cwbc_ref, cbx_ref, cbbc_ref,
                dtb_ref, alog_ref, dskip_ref, ng_ref, o_ref, xbuf, bcbuf, state, ybuf):
    c = pl.program_id(1)

    n_sub = xs_ref.shape[0] // CHUNK
    hist = slice(0, CHUNK)

    @pl.when(c == 0)
    def _():
        xbuf[hist, :] = jnp.zeros((CHUNK, SSD_INNER), BF16)
        bcbuf[hist, :] = jnp.zeros((CHUNK, 2 * SSD_BC), BF16)
        state[...] = jnp.zeros_like(state)

    xbuf[CHUNK:, :] = xs_ref[...]
    bcbuf[CHUNK:, :] = bc_ref[...]

    srow = lax.broadcasted_iota(jnp.int32, (CHUNK, 2 * CHUNK), 0)
    scol = lax.broadcasted_iota(jnp.int32, (CHUNK, 2 * CHUNK), 1)
    shifts = [jnp.where(scol == CHUNK + srow - back, 1.0, 0.0).astype(BF16) for back in range(1, SSD_CONV)]
    row = lax.broadcasted_iota(jnp.int32, (CHUNK, CHUNK), 0)
    col = lax.broadcasted_iota(jnp.int32, (CHUNK, CHUNK), 1)
    causal = row >= col
    tril = jnp.where(causal, 1.0, 0.0).astype(F32)
    lane_lo = lax.broadcasted_iota(jnp.int32, (CHUNK, LANES), 1) < SSD_HEAD_DIM
    a_neg = -jnp.exp(alog_ref[...])

    for sub in range(n_sub):
        rows = slice(sub * CHUNK, (sub + 1) * CHUNK)

        def conv_silu(buf, w_ref, b_ref):
            ext = buf[sub * CHUNK:(sub + 2) * CHUNK, :]
            acc = b_ref[...] + w_ref[SSD_CONV - 1:SSD_CONV, :] * ext[CHUNK:, :].astype(F32)
            for back in range(1, SSD_CONV):
                tap = SSD_CONV - 1 - back
                acc = acc + w_ref[tap:tap + 1, :] * _dot(shifts[back - 1], ext)
            return _silu(acc)

        xs = conv_silu(xbuf, cwx_ref, cbx_ref)
        bc = conv_silu(bcbuf, cwbc_ref, cbbc_ref)
        xs_b = xs.astype(BF16)

        v = dt_ref[rows, :] + dtb_ref[...]
        dt = jnp.maximum(v, 0.0) + jnp.log1p(jnp.exp(-jnp.abs(v)))
        da = dt * a_neg
        acs = jnp.dot(tril, da, preferred_element_type=F32, precision=lax.Precision.HIGHEST)
        acs2 = acs * LOG2E
        ldt = jnp.log2(dt)
        arow_t = (acs2 - ldt).T
        total2 = acs2[CHUNK - 1:CHUNK, :]
        w_t = jnp.exp2(total2 - acs2 + ldt).T

        for g in range(SSD_GROUPS):
            bm_g = bc[:, g * SSD_STATE:(g + 1) * SSD_STATE]
            cm_g = bc[:, SSD_BC + g * SSD_STATE:SSD_BC + (g + 1) * SSD_STATE]
            cb = _dot_nt(cm_g.astype(BF16), bm_g.astype(BF16))
            bm_gt = bm_g.T
            for pair in range(SSD_REP // 2):
                h0 = g * SSD_REP + 2 * pair
                ps = slice(h0 * SSD_HEAD_DIM, (h0 + 2) * SSD_HEAD_DIM)
                xs_p = xs_b[:, ps]
                st_old = state[h0 // 2]
                rhs = jnp.concatenate([xs_p, st_old.astype(BF16)], axis=0)
                ys, sts, cds = [], [], []
                for h in (h0, h0 + 1):
                    a_col = jnp.broadcast_to(acs2[:, h:h + 1], (CHUNK, CHUNK))
                    decay = jnp.exp2(jnp.where(causal, a_col - arow_t[h:h + 1, :], -jnp.inf))
                    e_col = jnp.exp2(a_col)
                    lhs = jnp.concatenate([(cb * decay).astype(BF16), (cm_g * e_col).astype(BF16)], axis=1)
                    ys.append(_dot(lhs, rhs))
                    sts.append(_dot((bm_gt * w_t[h:h + 1, :]).astype(BF16), xs_p))
                    cds.append(e_col[CHUNK - 1:CHUNK, :])
                ybuf[rows, ps] = jnp.where(lane_lo, ys[0], ys[1])
                chunk_decay = jnp.where(lane_lo[0:1, :], cds[0], cds[1])
                state[h0 // 2] = st_old * chunk_decay + jnp.where(lane_lo, sts[0], sts[1])

        y = (ybuf[rows, :] + xs * dskip_ref[...]) * _silu(z_ref[rows, :].astype(F32))
        r = lax.rsqrt(jnp.mean(y * y, axis=-1, keepdims=True) + NORM_EPS)
        o_ref[rows, :] = (y * r * ng_ref[...]).astype(o_ref.dtype)

    xbuf[hist, :] = xbuf[n_sub * CHUNK:, :]
    bcbuf[hist, :] = bcbuf[n_sub * CHUNK:, :]


def _ssd_branch(proj, dt_raw, conv_w, conv_b, dt_bias, a_log, d_skip, norm_g):
    b, s, _ = proj.shape
    rows = STEP_CHUNKS * CHUNK
    nc = s // rows

    def col(width, tile):
        return pl.BlockSpec((None, rows, width), lambda bi, ci: (bi, ci, tile * PROJ_TILE // width))

    def const(shape):
        return pl.BlockSpec(shape, lambda bi, ci: (0,) * len(shape))

    pad = LANES - SSD_HEADS
    dtb = jnp.pad(dt_bias, (0, pad))[None, :]
    alog = jnp.pad(a_log, (0, pad))[None, :]
    dskip = jnp.repeat(d_skip, SSD_HEAD_DIM)[None, :]
    return pl.pallas_call(
        _ssd_kernel,
        grid=(b, nc),
        in_specs=[col(SSD_INNER, T_Z), col(SSD_INNER, T_XS), col(2 * SSD_BC, T_BC),
                  pl.BlockSpec((None, rows, LANES), lambda bi, ci: (bi, ci, 0)),
                  const((SSD_CONV, SSD_INNER)), const((SSD_CONV, 2 * SSD_BC)),
                  const((1, SSD_INNER)), const((1, 2 * SSD_BC)),
                  const((1, LANES)), const((1, LANES)), const((1, SSD_INNER)), const((1, SSD_INNER))],
        out_specs=pl.BlockSpec((None, rows, SSD_INNER), lambda bi, ci: (bi, ci, 0)),
        out_shape=jax.ShapeDtypeStruct((b, s, SSD_INNER), BF16),
        scratch_shapes=[pltpu.VMEM((rows + CHUNK, SSD_INNER), BF16),
                        pltpu.VMEM((rows + CHUNK, 2 * SSD_BC), BF16),
                        pltpu.VMEM((SSD_HEADS // 2, SSD_STATE, 2 * SSD_HEAD_DIM), F32),
                        pltpu.VMEM((rows, SSD_INNER), F32)],
        compiler_params=_cparams(("parallel", "arbitrary")),
        name="ssd_mixer",
    )(proj, proj, proj, dt_raw, conv_w[:, :SSD_INNER], conv_w[:, SSD_INNER:],
      conv_b[None, :SSD_INNER], conv_b[None, SSD_INNER:], dtb, alog, dskip, norm_g[None, :])


def _rope_tables(s):
    inv_freq = ROPE_THETA ** (-jnp.arange(0, RET_QK_DIM, 2, dtype=F32) / RET_QK_DIM)
    ang = jnp.arange(s, dtype=F32)[:, None] * inv_freq[None, :]
    cos, sin = jnp.cos(ang), jnp.sin(ang)
    return jnp.concatenate([cos, cos], axis=-1), jnp.concatenate([-sin, sin], axis=-1)


_LOG_GAMMA = [float(np.log(np.float32(1.0) - np.exp2(np.float32(-5.0 - h)))) for h in range(RET_HEADS)]


def _ret_kernel(q_ref, k_ref, v_lo_ref, v_hi_ref, g_lo_ref, g_hi_ref, o_ref, state, dmask, qdec, kdec):
    c = pl.program_id(1)

    @pl.when(c == 0)
    def _():
        state[...] = jnp.zeros_like(state)
        li = lax.broadcasted_iota(jnp.int32, (CHUNK, CHUNK), 0).astype(F32)
        si = lax.broadcasted_iota(jnp.int32, (CHUNK, CHUNK), 1).astype(F32)
        rel = li - si
        for h in range(RET_HEADS):
            lg = _LOG_GAMMA[h]
            dmask[h] = jnp.where(rel >= 0, jnp.exp(jnp.maximum(rel, 0.0) * lg), 0.0)
            qdec[h] = jnp.exp((li + 1.0) * lg)
            kdec[h] = jnp.exp((CHUNK - 1.0 - li) * lg)

    for sub in range(q_ref.shape[0] // CHUNK):
        rows = slice(sub * CHUNK, (sub + 1) * CHUNK)
        for h in range(RET_HEADS):
            qs = slice(h * RET_QK_DIM, (h + 1) * RET_QK_DIM)
            vs = slice(h * RET_V_DIM, (h + 1) * RET_V_DIM)
            half = RET_HEADS // 2
            v_ref, g_ref = (v_lo_ref, g_lo_ref) if h < half else (v_hi_ref, g_hi_ref)
            hv = slice((h % half) * RET_V_DIM, (h % half + 1) * RET_V_DIM)
            qr = q_ref[rows, qs]
            kr = k_ref[rows, qs]
            v_h = v_ref[rows, hv]
            scores = _dot_nt(qr, kr) * dmask[h]
            inner = _dot(scores.astype(BF16), v_h)
            st_old = state[h]
            cross = _dot((qr.astype(F32) * qdec[h]).astype(BF16), st_old.astype(BF16))
            kv = _dot((kr.astype(F32) * kdec[h]).T.astype(BF16), v_h)
            state[h] = st_old * math.exp(CHUNK * _LOG_GAMMA[h]) + kv
            o = inner + cross
            o = o * lax.rsqrt(jnp.mean(o * o, axis=-1, keepdims=True) + NORM_EPS)
            o_ref[rows, vs] = (_silu(g_ref[rows, hv].astype(F32)) * o).astype(o_ref.dtype)


def _ret_branch(proj):
    b, s, _ = proj.shape
    rows = STEP_CHUNKS * CHUNK
    nc = s // rows

    def col(tile):
        return pl.BlockSpec((None, rows, PROJ_TILE), lambda bi, ci: (bi, ci, tile))

    return pl.pallas_call(
        _ret_kernel,
        grid=(b, nc),
        in_specs=[col(T_RQ), col(T_RK), col(T_RV), col(T_RV + 1), col(T_RG), col(T_RG + 1)],
        out_specs=pl.BlockSpec((None, rows, RET_V), lambda bi, ci: (bi, ci, 0)),
        out_shape=jax.ShapeDtypeStruct((b, s, RET_V), BF16),
        scratch_shapes=[pltpu.VMEM((RET_HEADS, RET_QK_DIM, RET_V_DIM), F32),
                        pltpu.VMEM((RET_HEADS, CHUNK, CHUNK), F32),
                        pltpu.VMEM((RET_HEADS, CHUNK, CHUNK), F32),
                        pltpu.VMEM((RET_HEADS, CHUNK, CHUNK), F32)],
        compiler_params=_cparams(("parallel", "arbitrary")),
        name="retention_mixer",
    )(proj, proj, proj, proj, proj, proj)


def _dil_kernel(q_ref, k_ref, v_ref, o_ref, lse_ref, kbuf, vbuf, *, blocks):
    n = pl.program_id(2)
    ext = 2 * DIL_HEAD_DIM
    first = slice(0, CHUNK)
    last = slice(blocks * CHUNK, (blocks + 1) * CHUNK)

    @pl.when(n == 0)
    def _():
        kbuf[first, :] = jnp.zeros((CHUNK, DIL_WIDTH), BF16)
        for h in range(DIL_HEADS):
            vbuf[first, h * ext:h * ext + DIL_HEAD_DIM] = jnp.zeros((CHUNK, DIL_HEAD_DIM), BF16)
            vbuf[:, h * ext + DIL_HEAD_DIM:(h + 1) * ext] = jnp.ones(((blocks + 1) * CHUNK, DIL_HEAD_DIM), BF16)

    parts, part_rows = q_ref.shape[0], q_ref.shape[1]
    piece = min(part_rows, CHUNK)

    def pieces(blk):
        return [((blk * CHUNK + off) // part_rows, (blk * CHUNK + off) % part_rows, off)
                for off in range(0, CHUNK, piece)]

    for pi in range(parts):
        dst = slice(CHUNK + pi * part_rows, CHUNK + (pi + 1) * part_rows)
        kbuf[dst, :] = k_ref[pi]
        for h in range(DIL_HEADS):
            vbuf[dst, h * ext:h * ext + DIL_HEAD_DIM] = v_ref[pi, :, h * DIL_HEAD_DIM:(h + 1) * DIL_HEAD_DIM]

    qi = lax.broadcasted_iota(jnp.int32, (CHUNK, 2 * CHUNK), 0)
    kj = lax.broadcasted_iota(jnp.int32, (CHUNK, 2 * CHUNK), 1)
    kl = kj & (CHUNK - 1)
    band = jnp.logical_or(jnp.logical_and(kj >= CHUNK, kl <= qi), jnp.logical_and(kj < CHUNK, kl >= qi))
    band_first = jnp.logical_and(band, jnp.logical_or(kj >= CHUNK, n > 0))
    lane = lax.broadcasted_iota(jnp.int32, (CHUNK, LANES), 1)

    for blk in range(blocks):
        keys = slice(blk * CHUNK, (blk + 2) * CHUNK)
        mask = band_first if blk == 0 else band
        block_pieces = pieces(blk)

        def q_head(cols):
            return jnp.concatenate([q_ref[pi, r0:r0 + piece, cols] for pi, r0, _ in block_pieces], axis=0)

        lse_all = jnp.zeros((CHUNK, LANES), F32)
        for h0 in range(0, DIL_HEADS, DIL_HEAD_GROUP):
            heads = range(h0, h0 + DIL_HEAD_GROUP)
            sl = {h: slice(h * DIL_HEAD_DIM, (h + 1) * DIL_HEAD_DIM) for h in heads}
            sc = {h: jnp.where(mask, _dot_nt(q_head(sl[h]), kbuf[keys, sl[h]]), -jnp.inf) for h in heads}
            mx = {h: jnp.max(sc[h], axis=-1, keepdims=True) for h in heads}
            pr = {h: jnp.exp(sc[h] - mx[h]).astype(BF16) for h in heads}
            res = {h: _dot(pr[h], vbuf[keys, h * ext:(h + 1) * ext]) for h in heads}
            for h in heads:
                den = res[h][:, DIL_HEAD_DIM:]
                o_h = (res[h][:, :DIL_HEAD_DIM] / den).astype(o_ref.dtype)
                for pi, r0, off in block_pieces:
                    o_ref[pi, r0:r0 + piece, sl[h]] = o_h[off:off + piece]
                lse_all = jnp.where(lane == h, mx[h] + jnp.log(den), lse_all)
        for pi, r0, off in block_pieces:
            lse_ref[pi, r0:r0 + piece, :] = lse_all[off:off + piece]

    kbuf[first, :] = kbuf[last, :]
    for h in range(DIL_HEADS):
        vbuf[first, h * ext:h * ext + DIL_HEAD_DIM] = vbuf[last, h * ext:h * ext + DIL_HEAD_DIM]


def _dil_group(qkv, tiles_qkv, dil, batch):
    tiles, _, rows, _ = qkv.shape
    seq_blocks = tiles * rows // batch // CHUNK
    blocks = min(DIL_MAX_BLOCKS, seq_blocks)
    assert rows <= blocks * CHUNK and seq_blocks % blocks == 0
    parts = blocks * CHUNK // rows
    nb = seq_blocks // blocks

    def part(width, tile):
        return pl.BlockSpec((parts, None, rows, width), lambda bi, r, n: (bi * nb + n, r, 0, tile))

    return pl.pallas_call(
        functools.partial(_dil_kernel, blocks=blocks),
        grid=(batch, dil, nb),
        in_specs=[part(DIL_WIDTH, t) for t in tiles_qkv],
        out_specs=[part(DIL_WIDTH, 0), part(LANES, 0)],
        out_shape=[jax.ShapeDtypeStruct((tiles, dil, rows, DIL_WIDTH), BF16),
                   jax.ShapeDtypeStruct((tiles, dil, rows, LANES), F32)],
        scratch_shapes=[pltpu.VMEM(((blocks + 1) * CHUNK, DIL_WIDTH), BF16),
                        pltpu.VMEM(((blocks + 1) * CHUNK, 2 * DIL_WIDTH), BF16)],
        compiler_params=_cparams(("parallel", "parallel", "arbitrary")),
        name=f"dilated_attention_d{dil}",
    )(qkv, qkv, qkv)


def _dil_combine_kernel(o0_ref, o1_ref, o2_ref, l0_ref, l1_ref, l2_ref, g_ref, y_ref, ob1, ob2, lb1, lb2):
    for (o_ref, l_ref, ob, lb, dil) in ((o1_ref, l1_ref, ob1, lb1, DIL_DILATIONS[1]),
                                        (o2_ref, l2_ref, ob2, lb2, DIL_DILATIONS[2])):
        rows = o_ref.shape[1]
        for r in range(dil):
            lb[pl.ds(r, rows, stride=dil), :] = l_ref[r]
            for h in range(DIL_HEADS):
                hs = slice(h * DIL_HEAD_DIM, (h + 1) * DIL_HEAD_DIM)
                ob[h, pl.ds(r, rows, stride=dil), :] = o_ref[r, :, hs].astype(F32)

    l0, l1, l2 = l0_ref[...], lb1[...], lb2[...]
    m = jnp.maximum(jnp.maximum(l0, l1), l2)
    e0, e1, e2 = jnp.exp(l0 - m), jnp.exp(l1 - m), jnp.exp(l2 - m)
    inv = 1.0 / (e0 + e1 + e2)
    w0, w1, w2 = e0 * inv, e1 * inv, e2 * inv
    rows = l0.shape[0]
    for h in range(DIL_HEADS):
        hs = slice(h * DIL_HEAD_DIM, (h + 1) * DIL_HEAD_DIM)

        def lane(w):
            return jnp.broadcast_to(w[:, h:h + 1], (rows, DIL_HEAD_DIM))

        o = lane(w0) * o0_ref[:, hs].astype(F32) + lane(w1) * ob1[h] + lane(w2) * ob2[h]
        y_ref[:, hs] = (_silu(g_ref[:, hs].astype(F32)) * o).astype(y_ref.dtype)


def _dil_combine(outs, lses, proj2d):
    m = proj2d.shape[0]
    d1, d2 = DIL_DILATIONS[1], DIL_DILATIONS[2]
    tm = CHUNK * d1
    o_spec = pl.BlockSpec((tm, DIL_WIDTH), lambda i: (i, 0))
    l_spec = pl.BlockSpec((tm, LANES), lambda i: (i, 0))

    def grouped_spec(arr, dil, width):
        sub = tm // dil
        per = arr.shape[2] // sub
        return pl.BlockSpec((None, dil, sub, width), lambda i: (i // per, 0, i % per, 0))

    def grouped(width):
        return grouped_spec(outs[1], d1, width), grouped_spec(outs[2], d2, width)

    o1_spec, o2_spec = grouped(DIL_WIDTH)
    l1_spec, l2_spec = grouped(LANES)
    return pl.pallas_call(
        _dil_combine_kernel,
        grid=(m // tm,),
        in_specs=[o_spec, o1_spec, o2_spec, l_spec, l1_spec, l2_spec,
                  pl.BlockSpec((tm, DIL_WIDTH), lambda i: (i, T_DG))],
        out_specs=o_spec,
        out_shape=jax.ShapeDtypeStruct((m, DIL_WIDTH), BF16),
        scratch_shapes=[pltpu.VMEM((DIL_HEADS, tm, DIL_HEAD_DIM), F32), pltpu.VMEM((DIL_HEADS, tm, DIL_HEAD_DIM), F32),
                        pltpu.VMEM((tm, LANES), F32), pltpu.VMEM((tm, LANES), F32)],
        compiler_params=_cparams(("parallel",)),
        name="dilated_combine",
    )(outs[0].reshape(m, DIL_WIDTH), outs[1], outs[2], lses[0].reshape(m, LANES), lses[1], lses[2], proj2d)


def _merge_kernel(ya_ref, yb_ref, yc_ref, wa_ref, wb_ref, wc_ref, ga_ref, gb_ref, gc_ref, o_ref):
    acc = _sigmoid(ga_ref[...].astype(F32)) * _dot(ya_ref[...], wa_ref[...])
    acc = acc + _sigmoid(gb_ref[...].astype(F32)) * _dot(yb_ref[...], wb_ref[...])
    acc = acc + _sigmoid(gc_ref[...].astype(F32)) * _dot(yc_ref[...], wc_ref[...])
    o_ref[...] = acc.astype(o_ref.dtype)


def _merge(ya, yb, yc, wa, wb, wc, proj2d, tm=512, tn=PROJ_TILE):
    m = ya.shape[0]
    d = wa.shape[1]

    def act(width):
        return pl.BlockSpec((tm, width), lambda j, i: (i, 0))

    def wgt(width):
        return pl.BlockSpec((width, tn), lambda j, i: (0, j))

    def gate(branch):
        return pl.BlockSpec((tm, tn), lambda j, i: (i, T_MG + branch * (d // tn) + j))

    return pl.pallas_call(
        _merge_kernel,
        grid=(d // tn, m // tm),
        in_specs=[act(SSD_INNER), act(RET_V), act(DIL_WIDTH), wgt(SSD_INNER), wgt(RET_V), wgt(DIL_WIDTH),
                  gate(0), gate(1), gate(2)],
        out_specs=pl.BlockSpec((tm, tn), lambda j, i: (i, j)),
        out_shape=jax.ShapeDtypeStruct((m, d), BF16),
        compiler_params=_cparams(("parallel", "parallel")),
        name="gated_merge",
    )(ya, yb, yc, wa, wb, wc, proj2d, proj2d, proj2d)


def _out_kernel(a_ref, w_ref, x_ref, g_ref, xo_ref, ho_ref):
    x = x_ref[...] + _dot(a_ref[...], w_ref[...])
    xo_ref[...] = x
    r = lax.rsqrt(jnp.mean(x * x, axis=-1, keepdims=True) + NORM_EPS)
    ho_ref[...] = (x * r * g_ref[...]).astype(ho_ref.dtype)


def _out_proj(merged, w_out, x2d, g_next, norm_dtype, tm=512):
    m, d = x2d.shape
    row = pl.BlockSpec((tm, d), lambda i: (i, 0))
    return pl.pallas_call(
        _out_kernel,
        grid=(m // tm,),
        in_specs=[row, pl.BlockSpec((d, d), lambda i: (0, 0)), row, pl.BlockSpec((1, d), lambda i: (0, 0))],
        out_specs=[row, row],
        out_shape=[jax.ShapeDtypeStruct((m, d), F32), jax.ShapeDtypeStruct((m, d), norm_dtype)],
        compiler_params=_cparams(("parallel",)),
        name="out_proj_residual_norm",
    )(merged, w_out, x2d, g_next.reshape(1, d))


def kernel(x, norm_g, w_in, conv_w, conv_b, dt_bias, a_log, d_skip, ssd_norm_g,
           w_o_ssd, w_o_ret, w_o_dil, w_out, final_norm_g):
    b, s, d = x.shape
    depth = w_in.shape[0]
    m = b * s
    assert s % (max(DIL_DILATIONS) * CHUNK) == 0 and s % PROJ_TILE == 0 and d == 2048
    cosf, sinf = _rope_tables(s)
    w_t = jnp.swapaxes(w_in, 1, 2)

    x2d = x.reshape(m, d)
    h = _rms_norm(x2d, norm_g[0], BF16)
    for layer in range(depth):
        proj2d = _projection(h, w_t, layer, cosf, sinf, n_tiles=N_MAIN_TILES, src_col=_main_src_col,
                             rope_tiles=(T_RQ, T_RK, T_DQ, T_DK), scaled_tiles=(T_RK, T_DQ), dil=1,
                             name="in_proj")
        dt_raw = _dt_projection(h, w_t, layer)
        proj = proj2d.reshape(b, s, N_MAIN)

        y_a = _ssd_branch(proj, dt_raw.reshape(b, s, LANES), conv_w[layer], conv_b[layer], dt_bias[layer],
                          a_log[layer], d_skip[layer], ssd_norm_g[layer])
        y_b = _ret_branch(proj)

        outs, lses = [], []
        for gi, dil in enumerate(DIL_DILATIONS):
            if dil == 1:
                qkv, tiles_qkv = proj2d.reshape(m // CHUNK, 1, CHUNK, N_MAIN), (T_DQ, T_DK, T_DV)
            else:
                first = W_DQ1_COL if gi == 1 else W_DQ2_COL
                qkv = _projection(h, w_t, layer, cosf, sinf, n_tiles=3,
                                  src_col=lambda j, f=first: (f // 8 + j * (PROJ_TILE // 8)) * 8,
                                  rope_tiles=(0, 1), scaled_tiles=(0,), dil=dil, name=f"dqkv_proj_d{dil}")
                tiles_qkv = (0, 1, 2)
            o, lse = _dil_group(qkv, tiles_qkv, dil, b)
            outs.append(o)
            lses.append(lse)
        y_c = _dil_combine(outs, lses, proj2d)

        merged = _merge(y_a.reshape(m, SSD_INNER), y_b.reshape(m, RET_V), y_c,
                        w_o_ssd[layer].astype(BF16), w_o_ret[layer].astype(BF16), w_o_dil[layer].astype(BF16),
                        proj2d)
        last = layer == depth - 1
        g_next = final_norm_g if last else norm_g[layer + 1]
        x2d, h = _out_proj(merged, w_out[layer].astype(BF16), x2d, g_next, F32 if last else BF16)
    return h.reshape(b, s, d)
```

```python
import functools
import math

import numpy as np
import jax
import jax.numpy as jnp
from jax import lax
from jax.experimental import pallas as pl
from jax.experimental.pallas import tpu as pltpu

F32 = jnp.float32
BF16 = jnp.bfloat16

NORM_EPS = 1e-6
CHUNK = 128
STEP_CHUNKS = 4
RET_STEP_CHUNKS = 8
ROPE_THETA = 10000.0
LOG2E = 1.4426950408889634

SSD_INNER = 2048
SSD_HEAD_DIM = 64
SSD_HEADS = 32
SSD_GROUPS = 4
SSD_REP = SSD_HEADS // SSD_GROUPS
SSD_STATE = 128
SSD_CONV = 4
SSD_BC = SSD_GROUPS * SSD_STATE

RET_HEADS = 8
RET_QK_DIM = 128
RET_V_DIM = 256
RET_QK = RET_HEADS * RET_QK_DIM
RET_V = RET_HEADS * RET_V_DIM

DIL_DILATIONS = (1, 4, 16)
DIL_HEADS = 8
DIL_HEAD_DIM = 128
DIL_WIDTH = DIL_HEADS * DIL_HEAD_DIM
DIL_HEAD_GROUP = 4
DIL_MAX_BLOCKS = 8
QK_SCALE = DIL_HEAD_DIM ** -0.5

LANES = 128
VMEM_LIMIT = 56 * 1024 * 1024
PROJ_TILE = 1024

W_DT_COL = 5120
W_AFTER_DT = W_DT_COL + SSD_HEADS
W_DQ1_COL = W_AFTER_DT + 9 * PROJ_TILE
W_DQ2_COL = W_AFTER_DT + 12 * PROJ_TILE

T_Z, T_XS, T_BC, T_RQ, T_RK, T_RV, T_RG, T_DQ, T_DK, T_DV, T_DG, T_MG = 0, 2, 4, 5, 6, 7, 9, 11, 12, 13, 14, 15
N_MAIN_TILES = 21
N_MAIN = N_MAIN_TILES * PROJ_TILE


def _main_src_col(j):
    sub = 8
    return (j * (PROJ_TILE // sub) + jnp.where(j < T_RQ, 0, SSD_HEADS // sub)
            + jnp.where(j < T_DG, 0, 6 * PROJ_TILE // sub)) * sub


def _cparams(sem):
    return pltpu.CompilerParams(dimension_semantics=sem, vmem_limit_bytes=VMEM_LIMIT)


def _sigmoid(v):
    return 1.0 / (1.0 + jnp.exp(-v))


def _silu(v):
    half = 0.5 * v
    return half + half * jnp.tanh(half)


def _dot(a, b):
    return jnp.dot(a, b, preferred_element_type=F32)


def _dot_nt(a, b):
    return lax.dot_general(a, b, (((1,), (1,)), ((), ())), preferred_element_type=F32)


def _norm_kernel(x_ref, g_ref, o_ref):
    x = x_ref[...]
    r = lax.rsqrt(jnp.mean(x * x, axis=-1, keepdims=True) + NORM_EPS)
    o_ref[...] = (x * r * g_ref[...]).astype(o_ref.dtype)


def _rms_norm(x2d, g, out_dtype, tm=512):
    m, d = x2d.shape
    return pl.pallas_call(
        _norm_kernel,
        grid=(m // tm,),
        in_specs=[pl.BlockSpec((tm, d), lambda i: (i, 0)),
                  pl.BlockSpec((1, d), lambda i: (0, 0))],
        out_specs=pl.BlockSpec((tm, d), lambda i: (i, 0)),
        out_shape=jax.ShapeDtypeStruct((m, d), out_dtype),
        compiler_params=_cparams(("parallel",)),
        name="rms_norm",
    )(x2d, g.reshape(1, d))


def _dt_kernel(a_ref, wt_ref, o_ref):
    res = _dot_nt(a_ref[...], wt_ref[...].astype(BF16))
    lane = lax.broadcasted_iota(jnp.int32, res.shape, 1)
    o_ref[...] = jnp.where(lane < SSD_HEADS, res, 0.0)


def _dt_projection(a, w_t, layer, tm=PROJ_TILE):
    m, k = a.shape
    return pl.pallas_call(
        _dt_kernel,
        grid=(m // tm,),
        in_specs=[pl.BlockSpec((tm, k), lambda i: (i, 0)),
                  pl.BlockSpec((None, LANES, k), lambda i: (layer, W_DT_COL // LANES, 0))],
        out_specs=pl.BlockSpec((tm, LANES), lambda i: (i, 0)),
        out_shape=jax.ShapeDtypeStruct((m, LANES), F32),
        compiler_params=_cparams(("parallel",)),
        name="dt_proj",
    )(a, w_t)


def _proj_kernel(a_ref, wt_ref, cos_ref, sin_ref, o_ref, wbuf, *scratch, rope_tiles, scaled_tiles, dil):
    j = pl.program_id(0)

    @pl.when(pl.program_id(1) == 0)
    def _():
        for c0 in range(0, wbuf.shape[0], LANES):
            wbuf[c0:c0 + LANES, :] = wt_ref[c0:c0 + LANES, :].astype(BF16)

    def any_of(tiles):
        hit = j == tiles[0]
        for t in tiles[1:]:
            hit = jnp.logical_or(hit, j == t)
        return hit

    def body(rope):
        if rope:
            factor = jnp.where(any_of(scaled_tiles), QK_SCALE, 1.0)
            cosf = cos_ref[...] * factor
            sinf = sin_ref[...] * factor
        res = _dot_nt(a_ref[...], wbuf[...])
        for c in range(res.shape[1] // LANES):
            cs = slice(c * LANES, (c + 1) * LANES)
            blk = res[:, cs]
            if rope:
                blk = blk * cosf + pltpu.roll(blk, DIL_HEAD_DIM // 2, 1) * sinf
            if dil == 1:
                o_ref[:, cs] = blk.astype(o_ref.dtype)
            else:
                acc = scratch[0]
                acc[c] = blk
                for r in range(dil):
                    o_ref[r, :, cs] = acc[c, pl.ds(r, o_ref.shape[1], stride=dil), :].astype(o_ref.dtype)

    is_rope = any_of(rope_tiles)
    pl.when(is_rope)(functools.partial(body, True))
    pl.when(jnp.logical_not(is_rope))(functools.partial(body, False))


def _projection(a, w_t, layer, cosf, sinf, *, n_tiles, src_col, rope_tiles, scaled_tiles, dil, name):
    m, k = a.shape
    s = cosf.shape[0]
    tn = PROJ_TILE
    n = n_tiles * tn
    tm = PROJ_TILE
    tab = pl.BlockSpec((tm, DIL_HEAD_DIM), lambda j, i: (i % (s // tm), 0))
    if dil == 1:
        out_spec = pl.BlockSpec((tm, tn), lambda j, i: (i, j))
        out_shape = jax.ShapeDtypeStruct((m, n), BF16)
        scratch = []
    else:
        out_spec = pl.BlockSpec((None, dil, tm // dil, tn), lambda j, i: (i, 0, 0, j))
        out_shape = jax.ShapeDtypeStruct((m // tm, dil, tm // dil, n), BF16)
        scratch = [pltpu.VMEM((tn // LANES, tm, LANES), F32)]
    return pl.pallas_call(
        functools.partial(_proj_kernel, rope_tiles=rope_tiles, scaled_tiles=scaled_tiles, dil=dil),
        grid=(n_tiles, m // tm),
        in_specs=[pl.BlockSpec((tm, k), lambda j, i: (i, 0)),
                  pl.BlockSpec((None, pl.Element(tn), pl.Element(k)), lambda j, i: (layer, src_col(j), 0)),
                  tab, tab],
        out_specs=out_spec,
        out_shape=out_shape,
        scratch_shapes=[pltpu.VMEM((tn, k), BF16)] + scratch,
        compiler_params=_cparams(("parallel", "arbitrary")),
        name=name,
    )(a, w_t, cosf, sinf)


def _ssd_kernel(z_ref, xs_ref, bc_ref, dt_ref, cwx_ref, cwbc_ref, cbx_ref, cbbc_ref,
                dtb_ref, alog_ref, dskip_ref, ng_ref, o_ref, xbuf, bcbuf, state, ybuf):
    c = pl.program_id(1)

    n_sub = xs_ref.shape[0] // CHUNK
    hist = slice(0, CHUNK)

    @pl.when(c == 0)
    def _():
        xbuf[hist, :] = jnp.zeros((CHUNK, SSD_INNER), BF16)
        bcbuf[hist, :] = jnp.zeros((CHUNK, 2 * SSD_BC), BF16)
        state[...] = jnp.zeros_like(state)

    xbuf[CHUNK:, :] = xs_ref[...]
    bcbuf[CHUNK:, :] = bc_ref[...]

    srow = lax.broadcasted_iota(jnp.int32, (CHUNK, 2 * CHUNK), 0)
    scol = lax.broadcasted_iota(jnp.int32, (CHUNK, 2 * CHUNK), 1)
    shifts = [jnp.where(scol == CHUNK + srow - back, 1.0, 0.0).astype(BF16) for back in range(1, SSD_CONV)]
    row = lax.broadcasted_iota(jnp.int32, (CHUNK, CHUNK), 0)
    col = lax.broadcasted_iota(jnp.int32, (CHUNK, CHUNK), 1)
    causal = row >= col
    tril = jnp.where(causal, 1.0, 0.0).astype(F32)
    lane_lo = lax.broadcasted_iota(jnp.int32, (CHUNK, LANES), 1) < SSD_HEAD_DIM
    a_neg = -jnp.exp(alog_ref[...])

    for sub in range(n_sub):
        rows = slice(sub * CHUNK, (sub + 1) * CHUNK)

        def conv_silu(buf, w_ref, b_ref):
            ext = buf[sub * CHUNK:(sub + 2) * CHUNK, :]
            acc = b_ref[...] + w_ref[SSD_CONV - 1:SSD_CONV, :] * ext[CHUNK:, :].astype(F32)
            for back in range(1, SSD_CONV):
                tap = SSD_CONV - 1 - back
                acc = acc + w_ref[tap:tap + 1, :] * _dot(shifts[back - 1], ext)
            return _silu(acc)

        xs = conv_silu(xbuf, cwx_ref, cbx_ref)
        bc = conv_silu(bcbuf, cwbc_ref, cbbc_ref)
        xs_b = xs.astype(BF16)

        v = dt_ref[rows, :] + dtb_ref[...]
        dt = jnp.maximum(v, 0.0) + jnp.log1p(jnp.exp(-jnp.abs(v)))
        da = dt * a_neg
        acs = jnp.dot(tril, da, preferred_element_type=F32, precision=lax.Precision.HIGHEST)
        acs2 = acs * LOG2E
        ldt = jnp.log2(dt)
        arow_t = (acs2 - ldt).T
        total2 = acs2[CHUNK - 1:CHUNK, :]
        w_t = jnp.exp2(total2 - acs2 + ldt).T

        for g in range(SSD_GROUPS):
            bm_g = bc[:, g * SSD_STATE:(g + 1) * SSD_STATE]
            cm_g = bc[:, SSD_BC + g * SSD_STATE:SSD_BC + (g + 1) * SSD_STATE]
            cb = _dot_nt(cm_g.astype(BF16), bm_g.astype(BF16))
            bm_gt = bm_g.T
            for pair in range(SSD_REP // 2):
                h0 = g * SSD_REP + 2 * pair
                ps = slice(h0 * SSD_HEAD_DIM, (h0 + 2) * SSD_HEAD_DIM)
                xs_p = xs_b[:, ps]
                st_old = state[h0 // 2]
                rhs = jnp.concatenate([xs_p, st_old.astype(BF16)], axis=0)
                ys, sts, cds = [], [], []
                for h in (h0, h0 + 1):
                    a_col = jnp.broadcast_to(acs2[:, h:h + 1], (CHUNK, CHUNK))
                    decay = jnp.exp2(jnp.where(causal, a_col - arow_t[h:h + 1, :], -jnp.inf))
                    e_col = jnp.exp2(a_col)
                    lhs = jnp.concatenate([(cb * decay).astype(BF16), (cm_g * e_col).astype(BF16)], axis=1)
                    ys.append(_dot(lhs, rhs))
                    sts.append(_dot((bm_gt * w_t[h:h + 1, :]).astype(BF16), xs_p))
                    cds.append(e_col[CHUNK - 1:CHUNK, :])
                ybuf[rows, ps] = jnp.where(lane_lo, ys[0], ys[1])
                chunk_decay = jnp.where(lane_lo[0:1, :], cds[0], cds[1])
                state[h0 // 2] = st_old * chunk_decay + jnp.where(lane_lo, sts[0], sts[1])

        y = (ybuf[rows, :] + xs * dskip_ref[...]) * _silu(z_ref[rows, :].astype(F32))
        r = lax.rsqrt(jnp.mean(y * y, axis=-1, keepdims=True) + NORM_EPS)
        o_ref[rows, :] = (y * r * ng_ref[...]).astype(o_ref.dtype)

    xbuf[hist, :] = xbuf[n_sub * CHUNK:, :]
    bcbuf[hist, :] = bcbuf[n_sub * CHUNK:, :]


def _ssd_branch(proj, dt_raw, conv_w, conv_b, dt_bias, a_log, d_skip, norm_g):
    b, s, _ = proj.shape
    rows = STEP_CHUNKS * CHUNK
    nc = s // rows

    def col(width, tile):
        return pl.BlockSpec((None, rows, width), lambda bi, ci: (bi, ci, tile * PROJ_TILE // width))

    def const(shape):
        return pl.BlockSpec(shape, lambda bi, ci: (0,) * len(shape))

    pad = LANES - SSD_HEADS
    dtb = jnp.pad(dt_bias, (0, pad))[None, :]
    alog = jnp.pad(a_log, (0, pad))[None, :]
    dskip = jnp.repeat(d_skip, SSD_HEAD_DIM)[None, :]
    return pl.pallas_call(
        _ssd_kernel,
        grid=(b, nc),
        in_specs=[col(SSD_INNER, T_Z), col(SSD_INNER, T_XS), col(2 * SSD_BC, T_BC),
                  pl.BlockSpec((None, rows, LANES), lambda bi, ci: (bi, ci, 0)),
                  const((SSD_CONV, SSD_INNER)), const((SSD_CONV, 2 * SSD_BC)),
                  const((1, SSD_INNER)), const((1, 2 * SSD_BC)),
                  const((1, LANES)), const((1, LANES)), const((1, SSD_INNER)), const((1, SSD_INNER))],
        out_specs=pl.BlockSpec((None, rows, SSD_INNER), lambda bi, ci: (bi, ci, 0)),
        out_shape=jax.ShapeDtypeStruct((b, s, SSD_INNER), BF16),
        scratch_shapes=[pltpu.VMEM((rows + CHUNK, SSD_INNER), BF16),
                        pltpu.VMEM((rows + CHUNK, 2 * SSD_BC), BF16),
                        pltpu.VMEM((SSD_HEADS // 2, SSD_STATE, 2 * SSD_HEAD_DIM), F32),
                        pltpu.VMEM((rows, SSD_INNER), F32)],
        compiler_params=_cparams(("parallel", "arbitrary")),
        name="ssd_mixer",
    )(proj, proj, proj, dt_raw, conv_w[:, :SSD_INNER], conv_w[:, SSD_INNER:],
      conv_b[None, :SSD_INNER], conv_b[None, SSD_INNER:], dtb, alog, dskip, norm_g[None, :])


def _rope_tables(s):
    inv_freq = ROPE_THETA ** (-jnp.arange(0, RET_QK_DIM, 2, dtype=F32) / RET_QK_DIM)
    ang = jnp.arange(s, dtype=F32)[:, None] * inv_freq[None, :]
    cos, sin = jnp.cos(ang), jnp.sin(ang)
    return jnp.concatenate([cos, cos], axis=-1), jnp.concatenate([-sin, sin], axis=-1)


_LOG_GAMMA = [float(np.log(np.float32(1.0) - np.exp2(np.float32(-5.0 - h)))) for h in range(RET_HEADS)]


def _ret_kernel(q_ref, k_ref, v_lo_ref, v_hi_ref, g_lo_ref, g_hi_ref, o_ref, state, dmask, qdec, kdec):
    c = pl.program_id(1)

    @pl.when(c == 0)
    def _():
        state[...] = jnp.zeros_like(state)
        li = lax.broadcasted_iota(jnp.int32, (CHUNK, CHUNK), 0).astype(F32)
        si = lax.broadcasted_iota(jnp.int32, (CHUNK, CHUNK), 1).astype(F32)
        rel = li - si
        for h in range(RET_HEADS):
            lg = _LOG_GAMMA[h]
            dmask[h] = jnp.where(rel >= 0, jnp.exp(jnp.maximum(rel, 0.0) * lg), 0.0)
            qdec[h] = jnp.exp((li + 1.0) * lg)
            kdec[h] = jnp.exp((CHUNK - 1.0 - li) * lg)

    for sub in range(q_ref.shape[0] // CHUNK):
        rows = slice(sub * CHUNK, (sub + 1) * CHUNK)
        for h in range(RET_HEADS):
            qs = slice(h * RET_QK_DIM, (h + 1) * RET_QK_DIM)
            vs = slice(h * RET_V_DIM, (h + 1) * RET_V_DIM)
            half = RET_HEADS // 2
            v_ref, g_ref = (v_lo_ref, g_lo_ref) if h < half else (v_hi_ref, g_hi_ref)
            hv = slice((h % half) * RET_V_DIM, (h % half + 1) * RET_V_DIM)
            qr = q_ref[rows, qs]
            kr = k_ref[rows, qs]
            v_h = v_ref[rows, hv]
            scores = _dot_nt(qr, kr) * dmask[h]
            inner = _dot(scores.astype(BF16), v_h)
            st_old = state[h]
            cross = _dot((qr.astype(F32) * qdec[h]).astype(BF16), st_old.astype(BF16))
            kv = _dot((kr.astype(F32) * kdec[h]).T.astype(BF16), v_h)
            state[h] = st_old * math.exp(CHUNK * _LOG_GAMMA[h]) + kv
            o = inner + cross
            o = o * lax.rsqrt(jnp.mean(o * o, axis=-1, keepdims=True) + NORM_EPS)
            o_ref[rows, vs] = (_silu(g_ref[rows, hv].astype(F32)) * o).astype(o_ref.dtype)


def _ret_branch(proj):
    b, s, _ = proj.shape
    rows = RET_STEP_CHUNKS * CHUNK
    nc = s // rows

    def col(tile):
        return pl.BlockSpec((None, rows, PROJ_TILE), lambda bi, ci: (bi, ci, tile))

    return pl.pallas_call(
        _ret_kernel,
        grid=(b, nc),
        in_specs=[col(T_RQ), col(T_RK), col(T_RV), col(T_RV + 1), col(T_RG), col(T_RG + 1)],
        out_specs=pl.BlockSpec((None, rows, RET_V), lambda bi, ci: (bi, ci, 0)),
        out_shape=jax.ShapeDtypeStruct((b, s, RET_V), BF16),
        scratch_shapes=[pltpu.VMEM((RET_HEADS, RET_QK_DIM, RET_V_DIM), F32),
                        pltpu.VMEM((RET_HEADS, CHUNK, CHUNK), F32),
                        pltpu.VMEM((RET_HEADS, CHUNK, CHUNK), F32),
                        pltpu.VMEM((RET_HEADS, CHUNK, CHUNK), F32)],
        compiler_params=_cparams(("parallel", "arbitrary")),
        name="retention_mixer",
    )(proj, proj, proj, proj, proj, proj)


def _dil_kernel(q_ref, k_ref, v_ref, o_ref, lse_ref, kbuf, vbuf, *, blocks):
    n = pl.program_id(2)
    ext = 2 * DIL_HEAD_DIM
    first = slice(0, CHUNK)
    last = slice(blocks * CHUNK, (blocks + 1) * CHUNK)

    @pl.when(n == 0)
    def _():
        kbuf[first, :] = jnp.zeros((CHUNK, DIL_WIDTH), BF16)
        for h in range(DIL_HEADS):
            vbuf[first, h * ext:h * ext + DIL_HEAD_DIM] = jnp.zeros((CHUNK, DIL_HEAD_DIM), BF16)
            vbuf[:, h * ext + DIL_HEAD_DIM:(h + 1) * ext] = jnp.ones(((blocks + 1) * CHUNK, DIL_HEAD_DIM), BF16)

    parts, part_rows = q_ref.shape[0], q_ref.shape[1]
    piece = min(part_rows, CHUNK)

    def pieces(blk):
        return [((blk * CHUNK + off) // part_rows, (blk * CHUNK + off) % part_rows, off)
                for off in range(0, CHUNK, piece)]

    for pi in range(parts):
        dst = slice(CHUNK + pi * part_rows, CHUNK + (pi + 1) * part_rows)
        kbuf[dst, :] = k_ref[pi]
        for h in range(DIL_HEADS):
            vbuf[dst, h * ext:h * ext + DIL_HEAD_DIM] = v_ref[pi, :, h * DIL_HEAD_DIM:(h + 1) * DIL_HEAD_DIM]

    qi = lax.broadcasted_iota(jnp.int32, (CHUNK, 2 * CHUNK), 0)
    kj = lax.broadcasted_iota(jnp.int32, (CHUNK, 2 * CHUNK), 1)
    kl = kj & (CHUNK - 1)
    band = jnp.logical_or(jnp.logical_and(kj >= CHUNK, kl <= qi), jnp.logical_and(kj < CHUNK, kl >= qi))
    band_first = jnp.logical_and(band, jnp.logical_or(kj >= CHUNK, n > 0))
    lane = lax.broadcasted_iota(jnp.int32, (CHUNK, LANES), 1)

    for blk in range(blocks):
        keys = slice(blk * CHUNK, (blk + 2) * CHUNK)
        mask = band_first if blk == 0 else band
        block_pieces = pieces(blk)

        def q_head(cols):
            return jnp.concatenate([q_ref[pi, r0:r0 + piece, cols] for pi, r0, _ in block_pieces], axis=0)

        lse_all = jnp.zeros((CHUNK, LANES), F32)
        for h0 in range(0, DIL_HEADS, DIL_HEAD_GROUP):
            heads = range(h0, h0 + DIL_HEAD_GROUP)
            sl = {h: slice(h * DIL_HEAD_DIM, (h + 1) * DIL_HEAD_DIM) for h in heads}
            sc = {h: jnp.where(mask, _dot_nt(q_head(sl[h]), kbuf[keys, sl[h]]), -jnp.inf) for h in heads}
            mx = {h: jnp.max(sc[h], axis=-1, keepdims=True) for h in heads}
            pr = {h: jnp.exp(sc[h] - mx[h]).astype(BF16) for h in heads}
            res = {h: _dot(pr[h], vbuf[keys, h * ext:(h + 1) * ext]) for h in heads}
            for h in heads:
                den = res[h][:, DIL_HEAD_DIM:]
                o_h = (res[h][:, :DIL_HEAD_DIM] / den).astype(o_ref.dtype)
                for pi, r0, off in block_pieces:
                    o_ref[pi, r0:r0 + piece, sl[h]] = o_h[off:off + piece]
                lse_all = jnp.where(lane == h, mx[h] + jnp.log(den), lse_all)
        for pi, r0, off in block_pieces:
            lse_ref[pi, r0:r0 + piece, :] = lse_all[off:off + piece]

    kbuf[first, :] = kbuf[last, :]
    for h in range(DIL_HEADS):
        vbuf[first, h * ext:h * ext + DIL_HEAD_DIM] = vbuf[last, h * ext:h * ext + DIL_HEAD_DIM]


def _dil_group(qkv, tiles_qkv, dil, batch):
    tiles, _, rows, _ = qkv.shape
    seq_blocks = tiles * rows // batch // CHUNK
    blocks = min(DIL_MAX_BLOCKS, seq_blocks)
    assert rows <= blocks * CHUNK and seq_blocks % blocks == 0
    parts = blocks * CHUNK // rows
    nb = seq_blocks // blocks

    def part(width, tile):
        return pl.BlockSpec((parts, None, rows, width), lambda bi, r, n: (bi * nb + n, r, 0, tile))

    return pl.pallas_call(
        functools.partial(_dil_kernel, blocks=blocks),
        grid=(batch, dil, nb),
        in_specs=[part(DIL_WIDTH, t) for t in tiles_qkv],
        out_specs=[part(DIL_WIDTH, 0), part(LANES, 0)],
        out_shape=[jax.ShapeDtypeStruct((tiles, dil, rows, DIL_WIDTH), BF16),
                   jax.ShapeDtypeStruct((tiles, dil, rows, LANES), F32)],
        scratch_shapes=[pltpu.VMEM(((blocks + 1) * CHUNK, DIL_WIDTH), BF16),
                        pltpu.VMEM(((blocks + 1) * CHUNK, 2 * DIL_WIDTH), BF16)],
        compiler_params=_cparams(("parallel", "parallel", "arbitrary")),
        name=f"dilated_attention_d{dil}",
    )(qkv, qkv, qkv)


def _dil_combine_kernel(o0_ref, o1_ref, o2_ref, l0_ref, l1_ref, l2_ref, g_ref, y_ref, ob1, ob2, lb1, lb2):
    for (o_ref, l_ref, ob, lb, dil) in ((o1_ref, l1_ref, ob1, lb1, DIL_DILATIONS[1]),
                                        (o2_ref, l2_ref, ob2, lb2, DIL_DILATIONS[2])):
        rows = o_ref.shape[1]
        for r in range(dil):
            lb[pl.ds(r, rows, stride=dil), :] = l_ref[r]
            for h in range(DIL_HEADS):
                hs = slice(h * DIL_HEAD_DIM, (h + 1) * DIL_HEAD_DIM)
                ob[h, pl.ds(r, rows, stride=dil), :] = o_ref[r, :, hs].astype(F32)

    l0, l1, l2 = l0_ref[...], lb1[...], lb2[...]
    m = jnp.maximum(jnp.maximum(l0, l1), l2)
    e0, e1, e2 = jnp.exp(l0 - m), jnp.exp(l1 - m), jnp.exp(l2 - m)
    inv = 1.0 / (e0 + e1 + e2)
    w0, w1, w2 = e0 * inv, e1 * inv, e2 * inv
    rows = l0.shape[0]
    for h in range(DIL_HEADS):
        hs = slice(h * DIL_HEAD_DIM, (h + 1) * DIL_HEAD_DIM)

        def lane(w):
            return jnp.broadcast_to(w[:, h:h + 1], (rows, DIL_HEAD_DIM))

        o = lane(w0) * o0_ref[:, hs].astype(F32) + lane(w1) * ob1[h] + lane(w2) * ob2[h]
        y_ref[:, hs] = (_silu(g_ref[:, hs].astype(F32)) * o).astype(y_ref.dtype)


def _dil_combine(outs, lses, proj2d):
    m = proj2d.shape[0]
    d1, d2 = DIL_DILATIONS[1], DIL_DILATIONS[2]
    tm = CHUNK * d1
    o_spec = pl.BlockSpec((tm, DIL_WIDTH), lambda i: (i, 0))
    l_spec = pl.BlockSpec((tm, LANES), lambda i: (i, 0))

    def grouped_spec(arr, dil, width):
        sub = tm // dil
        per = arr.shape[2] // sub
        return pl.BlockSpec((None, dil, sub, width), lambda i: (i // per, 0, i % per, 0))

    def grouped(width):
        return grouped_spec(outs[1], d1, width), grouped_spec(outs[2], d2, width)

    o1_spec, o2_spec = grouped(DIL_WIDTH)
    l1_spec, l2_spec = grouped(LANES)
    return pl.pallas_call(
        _dil_combine_kernel,
        grid=(m // tm,),
        in_specs=[o_spec, o1_spec, o2_spec, l_spec, l1_spec, l2_spec,
                  pl.BlockSpec((tm, DIL_WIDTH), lambda i: (i, T_DG))],
        out_specs=o_spec,
        out_shape=jax.ShapeDtypeStruct((m, DIL_WIDTH), BF16),
        scratch_shapes=[pltpu.VMEM((DIL_HEADS, tm, DIL_HEAD_DIM), F32), pltpu.VMEM((DIL_HEADS, tm, DIL_HEAD_DIM), F32),
                        pltpu.VMEM((tm, LANES), F32), pltpu.VMEM((tm, LANES), F32)],
        compiler_params=_cparams(("parallel",)),
        name="dilated_combine",
    )(outs[0].reshape(m, DIL_WIDTH), outs[1], outs[2], lses[0].reshape(m, LANES), lses[1], lses[2], proj2d)


def _merge_kernel(ya_ref, yb_ref, yc_ref, wa_ref, wb_ref, wc_ref, ga_ref, gb_ref, gc_ref, o_ref):
    acc = _sigmoid(ga_ref[...].astype(F32)) * _dot(ya_ref[...], wa_ref[...])
    acc = acc + _sigmoid(gb_ref[...].astype(F32)) * _dot(yb_ref[...], wb_ref[...])
    acc = acc + _sigmoid(gc_ref[...].astype(F32)) * _dot(yc_ref[...], wc_ref[...])
    o_ref[...] = acc.astype(o_ref.dtype)


def _merge(ya, yb, yc, wa, wb, wc, proj2d, tm=512, tn=PROJ_TILE):
    m = ya.shape[0]
    d = wa.shape[1]

    def act(width):
        return pl.BlockSpec((tm, width), lambda j, i: (i, 0))

    def wgt(width):
        return pl.BlockSpec((width, tn), lambda j, i: (0, j))

    def gate(branch):
        return pl.BlockSpec((tm, tn), lambda j, i: (i, T_MG + branch * (d // tn) + j))

    return pl.pallas_call(
        _merge_kernel,
        grid=(d // tn, m // tm),
        in_specs=[act(SSD_INNER), act(RET_V), act(DIL_WIDTH), wgt(SSD_INNER), wgt(RET_V), wgt(DIL_WIDTH),
                  gate(0), gate(1), gate(2)],
        out_specs=pl.BlockSpec((tm, tn), lambda j, i: (i, j)),
        out_shape=jax.ShapeDtypeStruct((m, d), BF16),
        compiler_params=_cparams(("parallel", "parallel")),
        name="gated_merge",
    )(ya, yb, yc, wa, wb, wc, proj2d, proj2d, proj2d)


def _out_kernel(a_ref, w_ref, x_ref, g_ref, xo_ref, ho_ref):
    x = x_ref[...] + _dot(a_ref[...], w_ref[...])
    xo_ref[...] = x
    r = lax.rsqrt(jnp.mean(x * x, axis=-1, keepdims=True) + NORM_EPS)
    ho_ref[...] = (x * r * g_ref[...]).astype(ho_ref.dtype)


def _out_proj(merged, w_out, x2d, g_next, norm_dtype, tm=512):
    m, d = x2d.shape
    row = pl.BlockSpec((tm, d), lambda i: (i, 0))
    return pl.pallas_call(
        _out_kernel,
        grid=(m // tm,),
        in_specs=[row, pl.BlockSpec((d, d), lambda i: (0, 0)), row, pl.BlockSpec((1, d), lambda i: (0, 0))],
        out_specs=[row, row],
        out_shape=[jax.ShapeDtypeStruct((m, d), F32), jax.ShapeDtypeStruct((m, d), norm_dtype)],
        compiler_params=_cparams(("parallel",)),
        name="out_proj_residual_norm",
    )(merged, w_out, x2d, g_next.reshape(1, d))


def kernel(x, norm_g, w_in, conv_w, conv_b, dt_bias, a_log, d_skip, ssd_norm_g,
           w_o_ssd, w_o_ret, w_o_dil, w_out, final_norm_g):
    b, s, d = x.shape
    depth = w_in.shape[0]
    m = b * s
    assert s % (max(DIL_DILATIONS) * CHUNK) == 0 and s % PROJ_TILE == 0 and d == 2048
    cosf, sinf = _rope_tables(s)
    w_t = jnp.swapaxes(w_in, 1, 2)

    x2d = x.reshape(m, d)
    h = _rms_norm(x2d, norm_g[0], BF16)
    for layer in range(depth):
        proj2d = _projection(h, w_t, layer, cosf, sinf, n_tiles=N_MAIN_TILES, src_col=_main_src_col,
                             rope_tiles=(T_RQ, T_RK, T_DQ, T_DK), scaled_tiles=(T_RK, T_DQ), dil=1,
                             name="in_proj")
        dt_raw = _dt_projection(h, w_t, layer)
        proj = proj2d.reshape(b, s, N_MAIN)

        y_a = _ssd_branch(proj, dt_raw.reshape(b, s, LANES), conv_w[layer], conv_b[layer], dt_bias[layer],
                          a_log[layer], d_skip[layer], ssd_norm_g[layer])
        y_b = _ret_branch(proj)

        outs, lses = [], []
        for gi, dil in enumerate(DIL_DILATIONS):
            if dil == 1:
                qkv, tiles_qkv = proj2d.reshape(m // CHUNK, 1, CHUNK, N_MAIN), (T_DQ, T_DK, T_DV)
            else:
                first = W_DQ1_COL if gi == 1 else W_DQ2_COL
                qkv = _projection(h, w_t, layer, cosf, sinf, n_tiles=3,
                                  src_col=lambda j, f=first: (f // 8 + j * (PROJ_TILE // 8)) * 8,
                                  rope_tiles=(0, 1), scaled_tiles=(0,), dil=dil, name=f"dqkv_proj_d{dil}")
                tiles_qkv = (0, 1, 2)
            o, lse = _dil_group(qkv, tiles_qkv, dil, b)
            outs.append(o)
            lses.append(lse)
        y_c = _dil_combine(outs, lses, proj2d)

        merged = _merge(y_a.reshape(m, SSD_INNER), y_b.reshape(m, RET_V), y_c,
                        w_o_ssd[layer].astype(BF16), w_o_ret[layer].astype(BF16), w_o_dil[layer].astype(BF16),
                        proj2d)
        last = layer == depth - 1
        g_next = final_norm_g if last else norm_g[layer + 1]
        x2d, h = _out_proj(merged, w_out[layer].astype(BF16), x2d, g_next, F32 if last else BF16)
    return h.reshape(b, s, d)
```
